```python
import math
import jax, jax.numpy as jnp
from jax import lax
import numpy as np

D_MODEL = 1024
BATCH = 8
SEQ = 2048
DEPTH = 2
DEC_BATCH = 128
DEC_SEQ = 4
PAST_LEN = 16384
PAGE_SIZE = 128

D_MIX = D_MODEL
D_POOL = D_MIX // 4
D_CONV = D_MIX // 4
D_MLA = D_MIX - D_POOL - D_CONV
POOL_WINDOWS = (2, 4, 8, 16)
POOL_GROUPS = len(POOL_WINDOWS)
POOL_GW = D_POOL // POOL_GROUPS
POOL_MAX = max(POOL_WINDOWS)
CONV_W = 3
N_HEADS = 4
V_DIM = D_MLA // N_HEADS
NOPE = 64
ROPE = 32
Q_LORA = 256
KV_LORA = 128
ROPE_THETA = 10000.0
QBLK = 128
EPS = 1e-6
NEG = -1e30
SM_SCALE = 1.0 / math.sqrt(NOPE + ROPE)
COL_WIDTHS = (D_POOL, D_POOL, D_CONV, D_CONV, D_CONV, D_CONV, Q_LORA, KV_LORA, ROPE, D_MLA)
D_IN = sum(COL_WIDTHS)

kernel_name = "hybrid_pool_conv_mla_decoder_step"


def rmsnorm(x, g):
    xf = x.astype(jnp.float32)
    y = xf * lax.rsqrt(jnp.mean(xf * xf, axis=-1, keepdims=True) + EPS)
    return (y * g.astype(jnp.float32)).astype(x.dtype)


def rope(x, pos):
    inv_freq = 1.0 / (ROPE_THETA ** (jnp.arange(0, ROPE, 2, dtype=jnp.float32) / ROPE))
    ang = pos.astype(jnp.float32)[:, None] * inv_freq[None, :]
    cos = jnp.concatenate([jnp.cos(ang), jnp.cos(ang)], axis=-1)
    sin = jnp.concatenate([jnp.sin(ang), jnp.sin(ang)], axis=-1)
    bshape = (1, pos.shape[0]) + (1,) * (x.ndim - 3) + (ROPE,)
    cos = cos.reshape(bshape)
    sin = sin.reshape(bshape)
    xf = x.astype(jnp.float32)
    x1, x2 = xf[..., : ROPE // 2], xf[..., ROPE // 2:]
    rot = jnp.concatenate([-x2, x1], axis=-1)
    return (xf * cos + rot * sin).astype(x.dtype)


def adaln(x, c, g, w_ada, b_ada):
    mod = jax.nn.silu(c) @ w_ada + b_ada
    shift, scale, gate = jnp.split(mod, 3, axis=-1)
    h = rmsnorm(x, g) * (1.0 + scale[:, None, :]) + shift[:, None, :]
    return h, gate


def split_columns(p):
    idx = []
    acc = 0
    for w in COL_WIDTHS[:-1]:
        acc += w
        idx.append(acc)
    return jnp.split(p, idx, axis=-1)


def pool_mix(u, prev, pos, pool_w, pool_scale):
    B, T, _ = u.shape
    u_ext = jnp.concatenate([prev, u], axis=1)
    cs = jnp.cumsum(u_ext.astype(jnp.float32), axis=1)
    cs = jnp.concatenate([jnp.zeros((B, 1, D_POOL), jnp.float32), cs], axis=1)
    means = []
    for gi, w in enumerate(POOL_WINDOWS):
        sl = slice(gi * POOL_GW, (gi + 1) * POOL_GW)
        wsum = cs[:, POOL_MAX:POOL_MAX + T, sl] - cs[:, POOL_MAX - w:POOL_MAX - w + T, sl]
        cnt = jnp.minimum(pos + 1, w).astype(jnp.float32)[None, :, None]
        means.append(wsum / cnt)
    d = (jnp.concatenate(means, axis=-1) - u.astype(jnp.float32)).astype(u.dtype)
    y = jnp.einsum('btgc,gcd->btgd', d.reshape(B, T, POOL_GROUPS, POOL_GW), pool_w)
    y = y.reshape(B, T, D_POOL) * pool_scale
    return y, u_ext[:, -(POOL_MAX - 1):]


def conv_mix(h, b_gate, c_gate, prev, conv_w):
    T = h.shape[1]
    v_ext = jnp.concatenate([prev, c_gate * h], axis=1)
    y = conv_w[0] * v_ext[:, 0:T]
    for k in range(1, CONV_W):
        y = y + conv_w[k] * v_ext[:, k:k + T]
    return b_gate * y, v_ext[:, -(CONV_W - 1):]


def mla_queries(cq_raw, pos, q_norm_g, w_uq, qn_g, qr_g):
    B, T, _ = cq_raw.shape
    q = (rmsnorm(cq_raw, q_norm_g) @ w_uq).reshape(B, T, N_HEADS, NOPE + ROPE)
    qn = rmsnorm(q[..., :NOPE], qn_g)
    qr = rope(rmsnorm(q[..., NOPE:], qr_g), pos)
    return qn, qr


def mla_latent(ckv_raw, kr_raw, pos, kv_norm_g, kr_g):
    ckv = rmsnorm(ckv_raw, kv_norm_g)
    kr = rope(rmsnorm(kr_raw, kr_g), pos)
    return ckv, kr


def key_nope(ckv, w_uk, kn_g):
    k = (ckv @ w_uk).reshape(ckv.shape[:-1] + (N_HEADS, NOPE))
    return rmsnorm(k, kn_g)


def attn_scores(qn, qr, kn, kr):
    s = jnp.einsum('bqhd,bkhd->bhqk', qn, kn) + jnp.einsum('bqhd,bkd->bhqk', qr, kr)
    return s.astype(jnp.float32) * SM_SCALE


def mla_prompt(qn, qr, ckv, kr, w_uk, w_uv, kn_g):
    B, T = qn.shape[:2]
    kn = key_nope(ckv, w_uk, kn_g)
    v = (ckv @ w_uv).reshape(B, T, N_HEADS, V_DIM)
    nb = T // QBLK
    qn_b = qn.reshape(B, nb, QBLK, N_HEADS, NOPE).swapaxes(0, 1)
    qr_b = qr.reshape(B, nb, QBLK, N_HEADS, ROPE).swapaxes(0, 1)
    kpos = jnp.arange(T)

    def block(args):
        qn_i, qr_i, i = args
        s = attn_scores(qn_i, qr_i, kn, kr)
        qpos = i * QBLK + jnp.arange(QBLK)
        s = jnp.where(kpos[None, :] <= qpos[:, None], s, NEG)
        p = jax.nn.softmax(s, axis=-1).astype(v.dtype)
        return jnp.einsum('bhqk,bkhv->bqhv', p, v)

    o = lax.map(block, (qn_b, qr_b, jnp.arange(nb)))
    return o.swapaxes(0, 1).reshape(B, T, D_MLA)


def mla_sample(qn, qr, ckv_new, kr_new, cache_latent, cache_krope, layer, page_table, w_uk, w_uv, kn_g):
    DB, T = qn.shape[:2]
    kn_new = key_nope(ckv_new, w_uk, kn_g)
    s = attn_scores(qn, qr, kn_new, kr_new)
    s = jnp.where(jnp.tril(jnp.ones((T, T), dtype=bool)), s, NEG)
    m = jnp.max(s, axis=-1)
    p = jnp.exp(s - m[..., None])
    lsum = jnp.sum(p, axis=-1)
    acc = jnp.einsum('bhqk,bkc->bhqc', p, ckv_new.astype(jnp.float32))

    def step(carry, pt):
        m, lsum, acc = carry
        lat = cache_latent[layer, pt]
        krp = cache_krope[layer, pt]
        kn = key_nope(lat, w_uk, kn_g)
        s = attn_scores(qn, qr, kn, krp)
        m_new = jnp.maximum(m, jnp.max(s, axis=-1))
        corr = jnp.exp(m - m_new)
        p = jnp.exp(s - m_new[..., None])
        lsum = lsum * corr + jnp.sum(p, axis=-1)
        acc = acc * corr[..., None] + jnp.einsum('bhqk,bkc->bhqc', p, lat.astype(jnp.float32))
        return (m_new, lsum, acc), None

    (m, lsum, acc), _ = lax.scan(step, (m, lsum, acc), page_table.T)
    lat_out = (acc / lsum[..., None]).astype(qn.dtype)
    o = jnp.einsum('bhqc,chv->bqhv', lat_out, w_uv.reshape(KV_LORA, N_HEADS, V_DIM))
    return o.reshape(DB, T, D_MLA)


def merge_out(x, gate, y_pool, z_pool, y_conv, z_conv, y_mla, z_mla, w_out):
    mixed = jnp.concatenate([y_pool * jax.nn.silu(z_pool),
                             y_conv * jax.nn.silu(z_conv),
                             y_mla * jax.nn.silu(z_mla)], axis=-1)
    return x + gate[:, None, :] * (mixed @ w_out)


def setup_inputs(seed: int = 0) -> dict:
    key = jax.random.key(seed)
    ks = jax.random.split(key, 32)
    n_pages = PAST_LEN // PAGE_SIZE
    used = DEC_BATCH * n_pages
    n_pool = used + max(1, used // 4)
    f32 = jnp.float32

    def nrm(k, shape, s=1.0):
        return jax.random.normal(k, shape, f32) * s

    def gain(k, shape):
        return 1.0 + 0.1 * jax.random.normal(k, shape, f32)

    page_table = jax.random.permutation(ks[6], n_pool)[:used].reshape(DEC_BATCH, n_pages).astype(jnp.int32)
    return {
        "x_prompt": nrm(ks[0], (BATCH, SEQ, D_MODEL)),
        "x_sample": nrm(ks[1], (DEC_BATCH, DEC_SEQ, D_MODEL)),
        "cache_latent": nrm(ks[2], (DEPTH, n_pool, PAGE_SIZE, KV_LORA)),
        "cache_krope": nrm(ks[3], (DEPTH, n_pool, PAGE_SIZE, ROPE)),
        "state_pool": nrm(ks[4], (DEPTH, DEC_BATCH, POOL_MAX - 1, D_POOL)),
        "state_conv": nrm(ks[5], (DEPTH, DEC_BATCH, CONV_W - 1, D_CONV)),
        "page_table": page_table,
        "c_prompt": nrm(ks[7], (BATCH, D_MODEL)),
        "c_sample": nrm(ks[8], (DEC_BATCH, D_MODEL)),
        "norm_g": gain(ks[9], (DEPTH, D_MODEL)),
        "w_ada": nrm(ks[10], (DEPTH, D_MODEL, 3 * D_MODEL), 0.5 * D_MODEL ** -0.5),
        "b_ada": nrm(ks[11], (DEPTH, 3 * D_MODEL), 0.01),
        "w_in": nrm(ks[12], (DEPTH, D_MODEL, D_IN), D_MODEL ** -0.5),
        "pool_w": nrm(ks[13], (DEPTH, POOL_GROUPS, POOL_GW, POOL_GW), POOL_GW ** -0.5),
        "pool_scale": gain(ks[14], (DEPTH, D_POOL)),
        "conv_w": nrm(ks[15], (DEPTH, CONV_W, D_CONV), CONV_W ** -0.5),
        "q_norm_g": gain(ks[16], (DEPTH, Q_LORA)),
        "w_uq": nrm(ks[17], (DEPTH, Q_LORA, N_HEADS * (NOPE + ROPE)), Q_LORA ** -0.5),
        "qn_g": gain(ks[18], (DEPTH, NOPE)),
        "qr_g": gain(ks[19], (DEPTH, ROPE)),
        "kv_norm_g": gain(ks[20], (DEPTH, KV_LORA)),
        "kr_g": gain(ks[21], (DEPTH, ROPE)),
        "w_uk": nrm(ks[22], (DEPTH, KV_LORA, N_HEADS * NOPE), KV_LORA ** -0.5),
        "kn_g": gain(ks[23], (DEPTH, NOPE)),
        "w_uv": nrm(ks[24], (DEPTH, KV_LORA, N_HEADS * V_DIM), KV_LORA ** -0.5),
        "w_out": nrm(ks[25], (DEPTH, D_MIX, D_MODEL), D_MIX ** -0.5),
    }


def reference(x_prompt, x_sample, cache_latent, cache_krope, state_pool, state_conv, page_table,
              c_prompt, c_sample, norm_g, w_ada, b_ada, w_in, pool_w, pool_scale, conv_w,
              q_norm_g, w_uq, qn_g, qr_g, kv_norm_g, kr_g, w_uk, kn_g, w_uv, w_out):
    past_len = page_table.shape[1] * cache_latent.shape[2]
    pos_p = jnp.arange(x_prompt.shape[1], dtype=jnp.int32)
    pos_s = past_len + jnp.arange(x_sample.shape[1], dtype=jnp.int32)
    bp = x_prompt.shape[0]

    xp = x_prompt
    lat_p, kr_p, pool_p, conv_p = [], [], [], []
    for l in range(DEPTH):
        h, gate = adaln(xp, c_prompt, norm_g[l], w_ada[l], b_ada[l])
        u, zp, hc, bg, cg, zc, cq, ckv_raw, kr_raw, zm = split_columns(h @ w_in[l])
        yp, st_pool = pool_mix(u, jnp.zeros((bp, POOL_MAX - 1, D_POOL), xp.dtype), pos_p,
                               pool_w[l], pool_scale[l])
        yc, st_conv = conv_mix(hc, bg, cg, jnp.zeros((bp, CONV_W - 1, D_CONV), xp.dtype), conv_w[l])
        qn, qr = mla_queries(cq, pos_p, q_norm_g[l], w_uq[l], qn_g[l], qr_g[l])
        ckv, kr = mla_latent(ckv_raw, kr_raw, pos_p, kv_norm_g[l], kr_g[l])
        ym = mla_prompt(qn, qr, ckv, kr, w_uk[l], w_uv[l], kn_g[l])
        xp = merge_out(xp, gate, yp, zp, yc, zc, ym, zm, w_out[l])
        lat_p.append(ckv)
        kr_p.append(kr)
        pool_p.append(st_pool)
        conv_p.append(st_conv)

    xs = x_sample
    lat_s, kr_s, pool_s, conv_s = [], [], [], []
    for l in range(DEPTH):
        h, gate = adaln(xs, c_sample, norm_g[l], w_ada[l], b_ada[l])
        u, zp, hc, bg, cg, zc, cq, ckv_raw, kr_raw, zm = split_columns(h @ w_in[l])
        yp, st_pool = pool_mix(u, state_pool[l], pos_s, pool_w[l], pool_scale[l])
        yc, st_conv = conv_mix(hc, bg, cg, state_conv[l], conv_w[l])
        qn, qr = mla_queries(cq, pos_s, q_norm_g[l], w_uq[l], qn_g[l], qr_g[l])
        ckv, kr = mla_latent(ckv_raw, kr_raw, pos_s, kv_norm_g[l], kr_g[l])
        ym = mla_sample(qn, qr, ckv, kr, cache_latent, cache_krope, l, page_table,
                        w_uk[l], w_uv[l], kn_g[l])
        xs = merge_out(xs, gate, yp, zp, yc, zc, ym, zm, w_out[l])
        lat_s.append(ckv)
        kr_s.append(kr)
        pool_s.append(st_pool)
        conv_s.append(st_conv)

    return (xp, xs,
            jnp.stack(lat_p), jnp.stack(kr_p), jnp.stack(pool_p), jnp.stack(conv_p),
            jnp.stack(lat_s), jnp.stack(kr_s), jnp.stack(pool_s), jnp.stack(conv_s))
```

```python
import functools
import math

import jax
import jax.numpy as jnp
from jax import lax
from jax.experimental import pallas as pl
from jax.experimental.pallas import tpu as pltpu

F32 = jnp.float32
BF16 = jnp.bfloat16

POOL_WINDOWS = (2, 4, 8, 16)
POOL_MAX = max(POOL_WINDOWS)
CONV_W = 3
N_HEADS = 4
NOPE = 64
ROPE = 32
ROPE_THETA = 10000.0
EPS = 1e-6
NEG = -1e30
SM_SCALE = 1.0 / math.sqrt(NOPE + ROPE)
LANES = 128
HALO = 16
VMEM_LIMIT = 56 * 1024 * 1024

C_POOL = 0
C_CONV = 512
C_MLA = 1536
C_ZM = 2048
D_IN_PAD = 2560


def _sigmoid(x):
    return 1.0 / (1.0 + jnp.exp(-x))


def _silu(x):
    return x * _sigmoid(x)


def _dot(a, b):
    return jnp.dot(a, b, preferred_element_type=F32)


def _dot_nt(a, b):
    return lax.dot_general(a, b, (((1,), (1,)), ((), ())), preferred_element_type=F32)


def _rms(x, n):
    return x * lax.rsqrt(jnp.sum(x * x, axis=-1, keepdims=True) * (1.0 / n) + EPS)


def _rope_block(x, cos_t, sin_a, sin_b):
    return x * cos_t + pltpu.roll(x, LANES - ROPE // 2, 1) * sin_a + pltpu.roll(x, ROPE // 2, 1) * sin_b


def _mod_kernel(c_ref, w_ref, b_ref, o_ref):
    a = _silu(c_ref[...]).astype(BF16)
    o_ref[0] = _dot(a, w_ref[0].astype(BF16)) + b_ref[0]


def _modulation(c_all, w_ada, b_ada):
    depth, d, d3 = w_ada.shape
    r = c_all.shape[0]
    nb = d3 // d
    return pl.pallas_call(
        _mod_kernel,
        out_shape=jax.ShapeDtypeStruct((depth, r, d3), F32),
        grid=(depth, nb),
        in_specs=[
            pl.BlockSpec((r, d), lambda l, j: (0, 0)),
            pl.BlockSpec((1, d, d), lambda l, j: (l, 0, j)),
            pl.BlockSpec((1, 1, d), lambda l, j: (l, 0, j)),
        ],
        out_specs=pl.BlockSpec((1, r, d), lambda l, j: (l, 0, j)),
        compiler_params=pltpu.CompilerParams(vmem_limit_bytes=VMEM_LIMIT),
        name="adaln_mod",
    )(c_all, w_ada, b_ada.reshape(depth, 1, d3))


def _mla_prep(pm, qng, wuq, s128q, gq, kvg, krg, cos_t, sin_a, sin_b):
    cq = pm[:, 0:256]
    ckv_raw = pm[:, 256:384]
    kr_raw = pm[:, 384:512]
    cqn = (_rms(cq, 256) * qng).astype(BF16)
    q = _dot(cqn, wuq)
    qh = []
    for h in range(N_HEADS):
        xh = q[:, h * LANES:(h + 1) * LANES]
        ms = _dot((xh * xh).astype(BF16), s128q)
        xh = xh * lax.rsqrt(ms + EPS) * gq
        qh.append(_rope_block(xh, cos_t, sin_a, sin_b))
    ckv = _rms(ckv_raw, 128) * kvg
    krn = _rms(kr_raw, ROPE) * krg
    kr = _rope_block(krn, cos_t, sin_a, sin_b)
    return qh, ckv, kr


def _prompt_in_kernel(x_ref, mod_ref, g_ref, win_ref, poolw_ref, pscale_ref, convw_ref,
                      qng_ref, wuq_ref, s128q_ref, gq_ref, kvg_ref, krg_ref,
                      wuk_ref, s128k_ref, gk_ref, wuv_ref, tab_ref,
                      gpc_ref, zmg_ref, qp_ref, kp_ref, v_ref, lat_ref, kr_ref, pst_ref, cst_ref,
                      uext, vext, *, ts, nt):
    t = pl.program_id(1)
    x = x_ref[0]
    ms = jnp.mean(x * x, axis=-1, keepdims=True)
    h = (x * lax.rsqrt(ms + EPS) * g_ref[...]) * (1.0 + mod_ref[0, 1:2, :]) + mod_ref[0, 0:1, :]
    hb = h.astype(BF16)

    @pl.when(t == 0)
    def _():
        uext[0:HALO, :] = jnp.zeros((HALO, 256), F32)
        vext[0:HALO, :] = jnp.zeros((HALO, 256), F32)

    pp = _dot(hb, win_ref[:, C_POOL:C_POOL + 512])
    u = pp[:, 0:256]
    zp = pp[:, 256:512]
    uext[HALO:HALO + ts, :] = u
    ext = uext[...]
    a1 = ext + pltpu.roll(ext, 1, 0)
    a2 = a1 + pltpu.roll(a1, 2, 0)
    a3 = a2[:, 128:256]
    a3 = a3 + pltpu.roll(a3, 4, 0)
    a4 = a3 + pltpu.roll(a3, 8, 0)
    pos1 = t * ts + lax.broadcasted_iota(jnp.int32, (ts, 1), 0) + 1
    inv = [1.0 / jnp.minimum(pos1, w).astype(F32) for w in POOL_WINDOWS]
    lane = lax.broadcasted_iota(jnp.int32, (1, LANES), 1)
    mean_l = jnp.where(lane < 64, a1[HALO:, 0:128] * inv[0], a2[HALO:, 0:128] * inv[1])
    mean_r = jnp.where(lane < 64, a3[HALO:] * inv[2], a4[HALO:] * inv[3])
    d = jnp.concatenate([mean_l, mean_r], axis=1) - u
    yp = _dot(d.astype(BF16), poolw_ref[...]) * pscale_ref[...]
    gp = yp * _silu(zp)

    @pl.when(t == nt - 1)
    def _():
        pst_ref[0] = uext[pl.ds(HALO + ts - (POOL_MAX - 1), POOL_MAX - 1), :]

    uext[0:HALO, :] = u[ts - HALO:ts, :]

    pc = _dot(hb, win_ref[:, C_CONV:C_CONV + 1024])
    hc = pc[:, 0:256]
    bg = pc[:, 256:512]
    cg = pc[:, 512:768]
    zc = pc[:, 768:1024]
    v = cg * hc
    vext[HALO:HALO + ts, :] = v
    e = vext[...]
    y = convw_ref[0:1, :] * pltpu.roll(e, 2, 0) + convw_ref[1:2, :] * pltpu.roll(e, 1, 0) + convw_ref[2:3, :] * e
    gc = bg * y[HALO:] * _silu(zc)
    gpc_ref[0] = jnp.concatenate([gp, gc], axis=1).astype(BF16)

    @pl.when(t == nt - 1)
    def _():
        cst_ref[0] = vext[pl.ds(HALO + ts - (CONV_W - 1), CONV_W - 1), :]

    vext[0:HALO, :] = v[ts - HALO:ts, :]

    zmg_ref[0] = _silu(_dot(hb, win_ref[:, C_ZM:C_ZM + 512])).astype(BF16)

    pm = _dot(hb, win_ref[:, C_MLA:C_MLA + 512])
    cos_t = tab_ref[0]
    sin_a = tab_ref[1]
    sin_b = tab_ref[2]
    qh, ckv, kr = _mla_prep(pm, qng_ref[...], wuq_ref[...], s128q_ref[...], gq_ref[...],
                            kvg_ref[...], krg_ref[...], cos_t, sin_a, sin_b)
    qp_ref[0] = jnp.concatenate(qh, axis=1).astype(BF16)
    lat_ref[0] = ckv
    kr_ref[0] = kr[:, NOPE:NOPE + ROPE]
    cb = ckv.astype(BF16)
    kc = _dot(cb, wuk_ref[...])
    kh = []
    for hh in range(N_HEADS):
        xh = kc[:, hh * LANES:(hh + 1) * LANES]
        msk = _dot((xh * xh).astype(BF16), s128k_ref[...])
        kh.append(xh * lax.rsqrt(msk + EPS) * gk_ref[...] + kr)
    kp_ref[0] = jnp.concatenate(kh, axis=1).astype(BF16)
    v_ref[0] = _dot(cb, wuv_ref[...]).astype(BF16)


def _prompt_in(x, mod_p, lw, tab, ts):
    b, t, d = x.shape
    nt = t // ts
    const = lambda shape: pl.BlockSpec(shape, lambda i, j: (0,) * len(shape))
    tile = lambda w: pl.BlockSpec((1, ts, w), lambda i, j: (i, j, 0))
    outs = [
        jax.ShapeDtypeStruct((b, t, 512), BF16),
        jax.ShapeDtypeStruct((b, t, 512), BF16),
        jax.ShapeDtypeStruct((b, t, 512), BF16),
        jax.ShapeDtypeStruct((b, t, 512), BF16),
        jax.ShapeDtypeStruct((b, t, 512), BF16),
        jax.ShapeDtypeStruct((b, t, 128), F32),
        jax.ShapeDtypeStruct((b, t, ROPE), F32),
        jax.ShapeDtypeStruct((b, POOL_MAX - 1, 256), F32),
        jax.ShapeDtypeStruct((b, CONV_W - 1, 256), F32),
    ]
    return pl.pallas_call(
        functools.partial(_prompt_in_kernel, ts=ts, nt=nt),
        out_shape=outs,
        grid=(b, nt),
        in_specs=[
            tile(d),
            pl.BlockSpec((1, 3, d), lambda i, j: (i, 0, 0)),
            const((1, d)),
            const((d, D_IN_PAD)),
            const((256, 256)), const((1, 256)), const((CONV_W, 256)),
            const((1, 256)), const((256, 512)), const((128, 128)), const((1, 128)),
            const((1, 128)), const((1, 128)),
            const((128, 512)), const((128, 128)), const((1, 128)), const((128, 512)),
            pl.BlockSpec((3, ts, 128), lambda i, j: (0, j, 0)),
        ],
        out_specs=[
            tile(512), tile(512), tile(512), tile(512), tile(512), tile(128), tile(ROPE),
            pl.BlockSpec((1, POOL_MAX - 1, 256), lambda i, j: (i, 0, 0)),
            pl.BlockSpec((1, CONV_W - 1, 256), lambda i, j: (i, 0, 0)),
        ],
        scratch_shapes=[pltpu.VMEM((HALO + ts, 256), F32), pltpu.VMEM((HALO + ts, 256), F32)],
        compiler_params=pltpu.CompilerParams(
            dimension_semantics=("parallel", "arbitrary"), vmem_limit_bytes=VMEM_LIMIT),
        name="prompt_in",
    )(x, mod_p, lw["norm_g"], lw["win"], lw["poolw"], lw["pscale"], lw["convw"],
      lw["qng"], lw["wuq"], lw["s128q"], lw["gq"], lw["kvg"], lw["krg"],
      lw["wuk"], lw["s128k"], lw["gk"], lw["wuv"], tab)


def _prompt_out_kernel(q_ref, k_ref, v_ref, gpc_ref, zmg_ref, x_ref, gate_ref, wout_ref, y_ref, *, tq):
    i = pl.program_id(1)
    row = lax.broadcasted_iota(jnp.int32, (tq, tq), 0)
    col = lax.broadcasted_iota(jnp.int32, (tq, tq), 1)
    diag_ok = col <= row
    heads = []
    for h in range(N_HEADS):
        sl = slice(h * LANES, (h + 1) * LANES)
        qh = q_ref[0, :, sl]

        def update(j, carry, masked):
            m, l, acc = carry
            start = pl.multiple_of(j * tq, tq)
            kh = k_ref[0, pl.ds(start, tq), sl]
            vh = v_ref[0, pl.ds(start, tq), sl]
            s = _dot_nt(qh, kh)
            if masked:
                s = jnp.where(diag_ok, s, NEG)
            m_new = jnp.maximum(m, jnp.max(s, axis=-1, keepdims=True))
            p = jnp.exp(s - m_new)
            corr = jnp.exp(m - m_new)
            l = l * corr + jnp.sum(p, axis=-1, keepdims=True)
            acc = acc * corr + _dot(p.astype(BF16), vh)
            return m_new, l, acc

        init = (jnp.full((tq, 1), NEG, F32), jnp.zeros((tq, 1), F32), jnp.zeros((tq, LANES), F32))
        carry = lax.fori_loop(0, i, functools.partial(update, masked=False), init)
        m, l, acc = update(i, carry, True)
        heads.append(acc / l)
    ym = jnp.concatenate(heads, axis=1) * zmg_ref[0].astype(F32)
    o = _dot(gpc_ref[0], wout_ref[0:512, :]) + _dot(ym.astype(BF16), wout_ref[512:1024, :])
    y_ref[0] = x_ref[0] + gate_ref[0, 2:3, :] * o


def _prompt_out(qp, kp, v, gpc, zmg, x, mod_p, wout, tq):
    b, t, d = x.shape
    tile = lambda w: pl.BlockSpec((1, tq, w), lambda i, j: (i, j, 0))
    full = lambda w: pl.BlockSpec((1, t, w), lambda i, j: (i, 0, 0))
    return pl.pallas_call(
        functools.partial(_prompt_out_kernel, tq=tq),
        out_shape=jax.ShapeDtypeStruct((b, t, d), F32),
        grid=(b, t // tq),
        in_specs=[
            tile(512), full(512), full(512), tile(512), tile(512), tile(d),
            pl.BlockSpec((1, 3, d), lambda i, j: (i, 0, 0)),
            pl.BlockSpec((d, d), lambda i, j: (0, 0)),
        ],
        out_specs=tile(d),
        compiler_params=pltpu.CompilerParams(
            dimension_semantics=("parallel", "arbitrary"), vmem_limit_bytes=VMEM_LIMIT),
        name="prompt_out",
    )(qp, kp, v, gpc, zmg, x, mod_p, wout)


def _sample_in_kernel(x_ref, mod_ref, g_ref, win_ref, poolw_ref, pscale_ref, convw_ref,
                      qng_ref, wuq_ref, s128q_ref, gq_ref, kvg_ref, krg_ref, kng_ref, wukt_ref, tab_ref,
                      pst_ref, cst_ref,
                      gpc_ref, zmg_ref, qabs_ref, qr_ref, lat_ref, krp_ref, psto_ref, csto_ref,
                      *, db, ts, d, past_len):
    shift = mod_ref[:, 0:d]
    scale1 = 1.0 + mod_ref[:, d:2 * d]
    hs = []
    for t in range(ts):
        x = x_ref[:, t * d:(t + 1) * d]
        ms = jnp.mean(x * x, axis=-1, keepdims=True)
        hs.append(((x * lax.rsqrt(ms + EPS) * g_ref[...]) * scale1 + shift).astype(BF16))
    hb = jnp.concatenate(hs, axis=0)
    rows = lambda a, t: a[t * db:(t + 1) * db]

    pp = _dot(hb, win_ref[:, C_POOL:C_POOL + 512])
    zp = pp[:, 256:512]
    np_ = POOL_MAX - 1
    ext = [pst_ref[:, k * 256:(k + 1) * 256] for k in range(np_)] + [rows(pp, t)[:, 0:256] for t in range(ts)]
    for k in range(np_):
        psto_ref[:, k * 256:(k + 1) * 256] = ext[ts + k]

    def doubled(prev, step, lo):
        return {r: prev[r] + prev[r - step] for r in range(lo, np_ + ts) if r in prev and (r - step) in prev}

    a0 = dict(enumerate(ext))
    a1 = doubled(a0, 1, 1)
    a2 = doubled(a1, 2, 3)
    a3 = doubled(a2, 4, 7)
    a4 = doubled(a3, 8, 15)
    lane = lax.broadcasted_iota(jnp.int32, (1, 256), 1)
    ds_ = []
    for t in range(ts):
        r = np_ + t
        inv = [1.0 / float(min(past_len + t + 1, w)) for w in POOL_WINDOWS]
        mean = jnp.where(lane < 64, a1[r] * inv[0],
                         jnp.where(lane < 128, a2[r] * inv[1],
                                   jnp.where(lane < 192, a3[r] * inv[2], a4[r] * inv[3])))
        ds_.append(mean - ext[r])
    dcat = jnp.concatenate(ds_, axis=0)
    gp = _dot(dcat.astype(BF16), poolw_ref[...]) * pscale_ref[...] * _silu(zp)

    pc = _dot(hb, win_ref[:, C_CONV:C_CONV + 1024])
    bg = pc[:, 256:512]
    zc = pc[:, 768:1024]
    vfull = pc[:, 512:768] * pc[:, 0:256]
    nc = CONV_W - 1
    vx = [cst_ref[:, k * 256:(k + 1) * 256] for k in range(nc)] + [rows(vfull, t) for t in range(ts)]
    for k in range(nc):
        csto_ref[:, k * 256:(k + 1) * 256] = vx[ts + k]
    ys = []
    for t in range(ts):
        acc = convw_ref[0:1, :] * vx[t]
        for k in range(1, CONV_W):
            acc = acc + convw_ref[k:k + 1, :] * vx[t + k]
        ys.append(acc)
    gc = bg * jnp.concatenate(ys, axis=0) * _silu(zc)
    gpc_ref[...] = jnp.concatenate([gp, gc], axis=1).astype(BF16)

    zmg_ref[...] = _silu(_dot(hb, win_ref[:, C_ZM:C_ZM + 512])).astype(BF16)

    pm = _dot(hb, win_ref[:, C_MLA:C_MLA + 512])
    qh, ckv, kr = _mla_prep(pm, qng_ref[...], wuq_ref[...], s128q_ref[...], gq_ref[...],
                            kvg_ref[...], krg_ref[...], tab_ref[0], tab_ref[1], tab_ref[2])
    for t in range(ts):
        lat_ref[:, t * LANES:(t + 1) * LANES] = rows(ckv, t)
        krp_ref[:, t * LANES:(t + 1) * LANES] = rows(kr, t)
    for h in range(N_HEADS):
        qa = _dot((qh[h] * kng_ref[...]).astype(BF16), wukt_ref[h])
        for t in range(ts):
            c0 = (h * ts + t) * LANES
            qabs_ref[:, c0:c0 + LANES] = rows(qa, t).astype(BF16)
            qr_ref[:, c0:c0 + LANES] = rows(qh[h], t).astype(BF16)


def _sample_in(xs2, mod_s, lw, tab, pst, cst, ts, past_len):
    db = xs2.shape[0]
    d = xs2.shape[1] // ts
    r = ts * db
    outs = [
        jax.ShapeDtypeStruct((r, 512), BF16),
        jax.ShapeDtypeStruct((r, 512), BF16),
        jax.ShapeDtypeStruct((db, N_HEADS * ts * LANES), BF16),
        jax.ShapeDtypeStruct((db, N_HEADS * ts * LANES), BF16),
        jax.ShapeDtypeStruct((db, ts * LANES), F32),
        jax.ShapeDtypeStruct((db, ts * LANES), F32),
        jax.ShapeDtypeStruct(pst.shape, F32),
        jax.ShapeDtypeStruct(cst.shape, F32),
    ]
    return pl.pallas_call(
        functools.partial(_sample_in_kernel, db=db, ts=ts, d=d, past_len=past_len),
        out_shape=outs,
        compiler_params=pltpu.CompilerParams(vmem_limit_bytes=VMEM_LIMIT),
        name="sample_in",
    )(xs2, mod_s, lw["norm_g"], lw["win"], lw["poolw"], lw["pscale"], lw["convw"],
      lw["qng"], lw["wuq"], lw["s128q"], lw["gq"], lw["kvg"], lw["krg"], lw["kngpad"], lw["wukt_pad"], tab,
      pst, cst)


def _sample_attn_kernel(pt_ref, qabs_ref, qr_ref, latn_ref, krn_ref, wukt_ref, *refs, g, ts, page):
    lat_refs = refs[0:g]
    kr_refs = refs[g:2 * g]
    out_ref = refs[2 * g]
    lhs, newlat, newkr, m_s, l_s, acc_s = refs[2 * g + 1:]
    c = pl.program_id(1)
    nq = N_HEADS * ts
    nk = N_HEADS * NOPE

    def scores(lat_bf, kr_bf, qr_bf):
        res = _dot_nt(lhs[...], lat_bf)
        kt = res[0:nk]
        k2 = kt * kt
        rr = []
        for h in range(N_HEADS):
            ss = jnp.sum(k2[h * NOPE:(h + 1) * NOPE], axis=0, keepdims=True)
            rr.append(jnp.broadcast_to(lax.rsqrt(ss * (1.0 / NOPE) + EPS), (ts, page)))
        return res[nk:nk + nq] * jnp.concatenate(rr, axis=0) + _dot_nt(qr_bf, kr_bf)

    qr_bf = qr_ref[0].astype(F32)[:, NOPE:NOPE + ROPE].astype(BF16)

    @pl.when(c == 0)
    def _():
        lhs[0:nk, :] = wukt_ref[...]
        lhs[nk:nk + nq, :] = qabs_ref[0]
        newlat[...] = jnp.zeros((page, LANES), F32)
        newkr[...] = jnp.zeros((page, ROPE), F32)
        newlat[0:ts, :] = latn_ref[0]
        newkr[0:ts, :] = krn_ref[0][:, NOPE:NOPE + ROPE]
        lat_bf = newlat[...].astype(BF16)
        s = scores(lat_bf, newkr[...].astype(BF16), qr_bf)
        key = lax.broadcasted_iota(jnp.int32, (nq, page), 1)
        qi = lax.broadcasted_iota(jnp.int32, (nq, page), 0) % ts
        s = jnp.where(key <= qi, s, NEG)
        m = jnp.max(s, axis=-1, keepdims=True)
        p = jnp.exp(s - m)
        m_s[...] = jnp.broadcast_to(m, (nq, LANES))
        l_s[...] = jnp.broadcast_to(jnp.sum(p, axis=-1, keepdims=True), (nq, LANES))
        acc_s[...] = _dot(p.astype(BF16), lat_bf)

    lat_bfs = [lat_refs[j][0, 0].astype(BF16) for j in range(g)]
    s = jnp.concatenate([scores(lat_bfs[j], kr_refs[j][0, 0].astype(BF16), qr_bf) for j in range(g)], axis=1)
    m_old = m_s[:, 0:1]
    m_new = jnp.maximum(m_old, jnp.max(s, axis=-1, keepdims=True))
    p = jnp.exp(s - m_new)
    corr = jnp.exp(m_old - m_new)
    l_new = l_s[:, 0:1] * corr + jnp.sum(p, axis=-1, keepdims=True)
    acc = acc_s[...] * corr + _dot(p.astype(BF16), jnp.concatenate(lat_bfs, axis=0))
    m_s[...] = jnp.broadcast_to(m_new, (nq, LANES))
    l_s[...] = jnp.broadcast_to(l_new, (nq, LANES))
    acc_s[...] = acc

    @pl.when(c == pl.num_programs(1) - 1)
    def _():
        out_ref[0] = acc / l_new


def _sample_attn(page_table, qabs, qr, latn, krn, wukt, cache_latent, cache_krope, layer, g, ts):
    db, n_pages = page_table.shape
    page = cache_latent.shape[2]
    nq = N_HEADS * ts
    per_seq = lambda rows, w: pl.BlockSpec((1, rows, w), lambda b, c, pt: (b, 0, 0))

    def page_spec(j, w):
        return pl.BlockSpec((1, 1, page, w), lambda b, c, pt: (layer, pt[b * n_pages + c * g + j], 0, 0))

    grid_spec = pltpu.PrefetchScalarGridSpec(
        num_scalar_prefetch=1,
        grid=(db, n_pages // g),
        in_specs=[per_seq(nq, LANES), per_seq(nq, LANES), per_seq(ts, LANES), per_seq(ts, LANES),
                  pl.BlockSpec((N_HEADS * NOPE, LANES), lambda b, c, pt: (0, 0))]
                 + [page_spec(j, LANES) for j in range(g)] + [page_spec(j, ROPE) for j in range(g)],
        out_specs=per_seq(nq, LANES),
        scratch_shapes=[
            pltpu.VMEM((N_HEADS * NOPE + nq, LANES), BF16),
            pltpu.VMEM((page, LANES), F32), pltpu.VMEM((page, ROPE), F32),
            pltpu.VMEM((nq, LANES), F32), pltpu.VMEM((nq, LANES), F32), pltpu.VMEM((nq, LANES), F32),
        ],
    )
    return pl.pallas_call(
        functools.partial(_sample_attn_kernel, g=g, ts=ts, page=page),
        out_shape=jax.ShapeDtypeStruct((db, nq, LANES), F32),
        grid_spec=grid_spec,
        compiler_params=pltpu.CompilerParams(
            dimension_semantics=("parallel", "arbitrary"), vmem_limit_bytes=VMEM_LIMIT),
        name="sample_attn",
    )(page_table.reshape(-1), qabs, qr, latn, krn, wukt, *([cache_latent] * g), *([cache_krope] * g))


def _sample_out_kernel(lo_ref, wuv_ref, gpc_ref, zmg_ref, x_ref, mod_ref, wout_ref, y_ref, *, db, ts, d):
    gate = mod_ref[:, 2 * d:3 * d]
    for t in range(ts):
        ym = jnp.concatenate(
            [_dot(lo_ref[:, (h * ts + t) * LANES:(h * ts + t + 1) * LANES].astype(BF16),
                  wuv_ref[:, h * LANES:(h + 1) * LANES]) for h in range(N_HEADS)], axis=1)
        ymg = (ym * zmg_ref[t * db:(t + 1) * db, :].astype(F32)).astype(BF16)
        o = _dot(gpc_ref[t * db:(t + 1) * db, :], wout_ref[0:512, :]) + _dot(ymg, wout_ref[512:1024, :])
        y_ref[:, t * d:(t + 1) * d] = x_ref[:, t * d:(t + 1) * d] + gate * o


def _sample_out(lat_out2, wuv, gpc, zmg, xs2, mod_s, wout, ts):
    db = xs2.shape[0]
    d = xs2.shape[1] // ts
    return pl.pallas_call(
        functools.partial(_sample_out_kernel, db=db, ts=ts, d=d),
        out_shape=jax.ShapeDtypeStruct(xs2.shape, F32),
        compiler_params=pltpu.CompilerParams(vmem_limit_bytes=VMEM_LIMIT),
        name="sample_out",
    )(lat_out2, wuv, gpc, zmg, xs2, mod_s, wout)


def _rope_tables(pos):
    inv_freq = 1.0 / (ROPE_THETA ** (jnp.arange(0, ROPE, 2, dtype=F32) / ROPE))
    ang = pos.astype(F32)[:, None] * inv_freq[None, :]
    cos, sin = jnp.cos(ang), jnp.sin(ang)
    n = pos.shape[0]
    half = ROPE // 2
    z = lambda w: jnp.zeros((n, w), F32)
    cos_t = jnp.concatenate([jnp.ones((n, NOPE), F32), cos, cos, jnp.ones((n, LANES - NOPE - ROPE), F32)], axis=1)
    sin_a = jnp.concatenate([z(NOPE), -sin, z(LANES - NOPE - half)], axis=1)
    sin_b = jnp.concatenate([z(NOPE + half), sin, z(LANES - NOPE - ROPE)], axis=1)
    return jnp.stack([cos_t, sin_a, sin_b])


def _layer_weights(l, norm_g, w_in, pool_w, pool_scale, conv_w, q_norm_g, w_uq, qn_g, qr_g,
                   kv_norm_g, kr_g, w_uk, kn_g, w_uv, w_out):
    d = w_in.shape[1]
    wi = w_in[l]
    zc = lambda w: jnp.zeros((d, w), F32)
    win = jnp.concatenate([wi[:, 0:1920], zc(NOPE), wi[:, 1920:1952], zc(LANES - NOPE - ROPE), wi[:, 1952:]], axis=1)
    pad_head = lambda w, per: jnp.concatenate(
        [jnp.concatenate([w[:, h * per:(h + 1) * per], jnp.zeros((w.shape[0], LANES - per), F32)], axis=1)
         for h in range(N_HEADS)], axis=1)
    seg = jnp.arange(LANES)
    nope_m = (seg < NOPE).astype(F32)
    rope_m = ((seg >= NOPE) & (seg < NOPE + ROPE)).astype(F32)
    s128k = nope_m[:, None] * nope_m[None, :] / NOPE
    s128q = s128k + rope_m[:, None] * rope_m[None, :] / ROPE
    zl = lambda w: jnp.zeros((w,), F32)
    pw = pool_w[l]
    gw = pw.shape[1]
    poolw = jnp.zeros((256, 256), F32)
    for gi in range(pw.shape[0]):
        poolw = poolw.at[gi * gw:(gi + 1) * gw, gi * gw:(gi + 1) * gw].set(pw[gi])
    wukt_pad = jnp.stack([jnp.concatenate([w_uk[l][:, h * NOPE:(h + 1) * NOPE].T,
                                           jnp.zeros((LANES - NOPE, w_uk.shape[1]), F32)], axis=0)
                          for h in range(N_HEADS)])
    return {
        "norm_g": norm_g[l][None, :],
        "win": win.astype(BF16),
        "poolw": poolw.astype(BF16),
        "pscale": pool_scale[l][None, :],
        "convw": conv_w[l],
        "qng": q_norm_g[l][None, :],
        "wuq": pad_head(w_uq[l], NOPE + ROPE).astype(BF16),
        "s128q": s128q.astype(BF16),
        "s128k": s128k.astype(BF16),
        "gq": (jnp.concatenate([qn_g[l], qr_g[l], zl(LANES - NOPE - ROPE)]) * SM_SCALE)[None, :],
        "kvg": kv_norm_g[l][None, :],
        "krg": jnp.concatenate([zl(NOPE), kr_g[l], zl(LANES - NOPE - ROPE)])[None, :],
        "wuk": pad_head(w_uk[l], NOPE).astype(BF16),
        "gk": jnp.concatenate([kn_g[l], zl(LANES - NOPE)])[None, :],
        "kngpad": jnp.concatenate([kn_g[l], zl(LANES - NOPE)])[None, :],
        "wuv": w_uv[l].astype(BF16),
        "wukt": w_uk[l].T.astype(BF16),
        "wukt_pad": wukt_pad.astype(BF16),
        "wout": w_out[l].astype(BF16),
    }


def kernel(x_prompt, x_sample, cache_latent, cache_krope, state_pool, state_conv, page_table,
           c_prompt, c_sample, norm_g, w_ada, b_ada, w_in, pool_w, pool_scale, conv_w,
           q_norm_g, w_uq, qn_g, qr_g, kv_norm_g, kr_g, w_uk, kn_g, w_uv, w_out):
    depth = w_in.shape[0]
    bp, seq, d = x_prompt.shape
    db, ts, _ = x_sample.shape
    page = cache_latent.shape[2]
    past_len = page_table.shape[1] * page
    ts_p = 512 if seq % 512 == 0 else seq
    tq = 256 if seq % 256 == 0 else seq
    g_pages = 8

    mod = _modulation(jnp.concatenate([c_prompt, c_sample], axis=0), w_ada, b_ada)
    tab_p = _rope_tables(jnp.arange(seq, dtype=jnp.int32))
    tab_s = jnp.repeat(_rope_tables(past_len + jnp.arange(ts, dtype=jnp.int32)), db, axis=1)
    lws = [_layer_weights(l, norm_g, w_in, pool_w, pool_scale, conv_w, q_norm_g, w_uq, qn_g, qr_g,
                          kv_norm_g, kr_g, w_uk, kn_g, w_uv, w_out) for l in range(depth)]

    xp = x_prompt
    lat_p, kr_p, pool_p, conv_p = [], [], [], []
    for l in range(depth):
        mod_p = mod[l, :bp].reshape(bp, 3, d)
        gpc, zmg, qp, kp, v, lat, kr, pst, cst = _prompt_in(xp, mod_p, lws[l], tab_p, ts_p)
        xp = _prompt_out(qp, kp, v, gpc, zmg, xp, mod_p, lws[l]["wout"], tq)
        lat_p.append(lat)
        kr_p.append(kr)
        pool_p.append(pst)
        conv_p.append(cst)

    xs2 = x_sample.reshape(db, ts * d)
    lat_s, kr_s, pool_s, conv_s = [], [], [], []
    for l in range(depth):
        mod_s = mod[l, bp:]
        pst = state_pool[l].reshape(db, -1)
        cst = state_conv[l].reshape(db, -1)
        gpc, zmg, qabs, qr, latn, krn, psto, csto = _sample_in(xs2, mod_s, lws[l], tab_s, pst, cst, ts, past_len)
        nq = N_HEADS * ts
        lat_out = _sample_attn(page_table, qabs.reshape(db, nq, LANES), qr.reshape(db, nq, LANES),
                               latn.reshape(db, ts, LANES), krn.reshape(db, ts, LANES), lws[l]["wukt"],
                               cache_latent, cache_krope, l, g_pages, ts)
        xs2 = _sample_out(lat_out.reshape(db, nq * LANES), lws[l]["wuv"], gpc, zmg, xs2, mod_s,
                          lws[l]["wout"], ts)
        lat_s.append(latn.reshape(db, ts, LANES))
        kr_s.append(krn.reshape(db, ts, LANES)[:, :, NOPE:NOPE + ROPE])
        pool_s.append(psto.reshape(state_pool.shape[1:]))
        conv_s.append(csto.reshape(state_conv.shape[1:]))

    return (xp, xs2.reshape(x_sample.shape),
            jnp.stack(lat_p), jnp.stack(kr_p), jnp.stack(pool_p), jnp.stack(conv_p),
            jnp.stack(lat_s), jnp.stack(kr_s), jnp.stack(pool_s), jnp.stack(conv_s))
```

```python
import functools
import math

import jax
import jax.numpy as jnp
from jax import lax
from jax.experimental import pallas as pl
from jax.experimental.pallas import tpu as pltpu

F32 = jnp.float32
BF16 = jnp.bfloat16

POOL_WINDOWS = (2, 4, 8, 16)
POOL_MAX = max(POOL_WINDOWS)
CONV_W = 3
N_HEADS = 4
NOPE = 64
ROPE = 32
ROPE_THETA = 10000.0
EPS = 1e-6
NEG = -1e30
SM_SCALE = 1.0 / math.sqrt(NOPE + ROPE)
LANES = 128
HALO = 16
VMEM_LIMIT = 56 * 1024 * 1024

C_POOL = 0
C_CONV = 512
C_MLA = 1536
C_ZM = 2048
D_IN_PAD = 2560


def _sigmoid(x):
    return 1.0 / (1.0 + jnp.exp(-x))


def _silu(x):
    return x * _sigmoid(x)


def _dot(a, b):
    return jnp.dot(a, b, preferred_element_type=F32)


def _dot_nt(a, b):
    return lax.dot_general(a, b, (((1,), (1,)), ((), ())), preferred_element_type=F32)


def _rms(x, n):
    return x * lax.rsqrt(jnp.sum(x * x, axis=-1, keepdims=True) * (1.0 / n) + EPS)


def _rope_block(x, cos_t, sin_a, sin_b):
    return x * cos_t + pltpu.roll(x, LANES - ROPE // 2, 1) * sin_a + pltpu.roll(x, ROPE // 2, 1) * sin_b


def _mod_kernel(c_ref, w_ref, b_ref, o_ref):
    a = _silu(c_ref[...]).astype(BF16)
    o_ref[0] = _dot(a, w_ref[0].astype(BF16)) + b_ref[0]


def _modulation(c_all, w_ada, b_ada):
    depth, d, d3 = w_ada.shape
    r = c_all.shape[0]
    nb = d3 // d
    return pl.pallas_call(
        _mod_kernel,
        out_shape=jax.ShapeDtypeStruct((depth, r, d3), F32),
        grid=(depth, nb),
        in_specs=[
            pl.BlockSpec((r, d), lambda l, j: (0, 0)),
            pl.BlockSpec((1, d, d), lambda l, j: (l, 0, j)),
            pl.BlockSpec((1, 1, d), lambda l, j: (l, 0, j)),
        ],
        out_specs=pl.BlockSpec((1, r, d), lambda l, j: (l, 0, j)),
        compiler_params=pltpu.CompilerParams(vmem_limit_bytes=VMEM_LIMIT),
        name="adaln_mod",
    )(c_all, w_ada, b_ada.reshape(depth, 1, d3))


def _mla_prep(pm, qng, wuq, s128q, gq, kvg, krg, cos_t, sin_a, sin_b):
    cq = pm[:, 0:256]
    ckv_raw = pm[:, 256:384]
    kr_raw = pm[:, 384:512]
    cqn = (_rms(cq, 256) * qng).astype(BF16)
    q = _dot(cqn, wuq)
    qh = []
    for h in range(N_HEADS):
        xh = q[:, h * LANES:(h + 1) * LANES]
        ms = _dot((xh * xh).astype(BF16), s128q)
        xh = xh * lax.rsqrt(ms + EPS) * gq
        qh.append(_rope_block(xh, cos_t, sin_a, sin_b))
    ckv = _rms(ckv_raw, 128) * kvg
    krn = _rms(kr_raw, ROPE) * krg
    kr = _rope_block(krn, cos_t, sin_a, sin_b)
    return qh, ckv, kr


def _prompt_in_kernel(x_ref, mod_ref, g_ref, win_ref, poolw_ref, pscale_ref, convw_ref,
                      qng_ref, wuq_ref, s128q_ref, gq_ref, kvg_ref, krg_ref,
                      wuk_ref, s128k_ref, gk_ref, wuv_ref, tab_ref,
                      gpc_ref, zmg_ref, qp_ref, kp_ref, v_ref, lat_ref, kr_ref, pst_ref, cst_ref,
                      uext, vext, *, ts, nt):
    t = pl.program_id(1)
    x = x_ref[0]
    ms = jnp.mean(x * x, axis=-1, keepdims=True)
    h = (x * lax.rsqrt(ms + EPS) * g_ref[...]) * (1.0 + mod_ref[0, 1:2, :]) + mod_ref[0, 0:1, :]
    hb = h.astype(BF16)

    @pl.when(t == 0)
    def _():
        uext[0:HALO, :] = jnp.zeros((HALO, 256), F32)
        vext[0:HALO, :] = jnp.zeros((HALO, 256), F32)

    pp = _dot(hb, win_ref[:, C_POOL:C_POOL + 512])
    u = pp[:, 0:256]
    zp = pp[:, 256:512]
    uext[HALO:HALO + ts, :] = u
    ext = uext[...]
    a1 = ext + pltpu.roll(ext, 1, 0)
    a2 = a1 + pltpu.roll(a1, 2, 0)
    a3 = a2[:, 128:256]
    a3 = a3 + pltpu.roll(a3, 4, 0)
    a4 = a3 + pltpu.roll(a3, 8, 0)
    pos1 = t * ts + lax.broadcasted_iota(jnp.int32, (ts, 1), 0) + 1
    inv = [1.0 / jnp.minimum(pos1, w).astype(F32) for w in POOL_WINDOWS]
    lane = lax.broadcasted_iota(jnp.int32, (1, LANES), 1)
    mean_l = jnp.where(lane < 64, a1[HALO:, 0:128] * inv[0], a2[HALO:, 0:128] * inv[1])
    mean_r = jnp.where(lane < 64, a3[HALO:] * inv[2], a4[HALO:] * inv[3])
    d = jnp.concatenate([mean_l, mean_r], axis=1) - u
    yp = _dot(d.astype(BF16), poolw_ref[...]) * pscale_ref[...]
    gp = yp * _silu(zp)

    @pl.when(t == nt - 1)
    def _():
        pst_ref[0] = uext[pl.ds(HALO + ts - (POOL_MAX - 1), POOL_MAX - 1), :]

    uext[0:HALO, :] = u[ts - HALO:ts, :]

    pc = _dot(hb, win_ref[:, C_CONV:C_CONV + 1024])
    hc = pc[:, 0:256]
    bg = pc[:, 256:512]
    cg = pc[:, 512:768]
    zc = pc[:, 768:1024]
    v = cg * hc
    vext[HALO:HALO + ts, :] = v
    e = vext[...]
    y = convw_ref[0:1, :] * pltpu.roll(e, 2, 0) + convw_ref[1:2, :] * pltpu.roll(e, 1, 0) + convw_ref[2:3, :] * e
    gc = bg * y[HALO:] * _silu(zc)
    gpc_ref[0] = jnp.concatenate([gp, gc], axis=1).astype(BF16)

    @pl.when(t == nt - 1)
    def _():
        cst_ref[0] = vext[pl.ds(HALO + ts - (CONV_W - 1), CONV_W - 1), :]

    vext[0:HALO, :] = v[ts - HALO:ts, :]

    zmg_ref[0] = _silu(_dot(hb, win_ref[:, C_ZM:C_ZM + 512])).astype(BF16)

    pm = _dot(hb, win_ref[:, C_MLA:C_MLA + 512])
    cos_t = tab_ref[0]
    sin_a = tab_ref[1]
    sin_b = tab_ref[2]
    qh, ckv, kr = _mla_prep(pm, qng_ref[...], wuq_ref[...], s128q_ref[...], gq_ref[...],
                            kvg_ref[...], krg_ref[...], cos_t, sin_a, sin_b)
    qp_ref[0] = jnp.concatenate(qh, axis=1).astype(BF16)
    lat_ref[0] = ckv
    kr_ref[0] = kr[:, NOPE:NOPE + ROPE]
    cb = ckv.astype(BF16)
    kc = _dot(cb, wuk_ref[...])
    kh = []
    for hh in range(N_HEADS):
        xh = kc[:, hh * LANES:(hh + 1) * LANES]
        msk = _dot((xh * xh).astype(BF16), s128k_ref[...])
        kh.append(xh * lax.rsqrt(msk + EPS) * gk_ref[...] + kr)
    kp_ref[0] = jnp.concatenate(kh, axis=1).astype(BF16)
    v_ref[0] = _dot(cb, wuv_ref[...]).astype(BF16)


def _prompt_in(x, mod_p, lw, tab, ts):
    b, t, d = x.shape
    nt = t // ts
    const = lambda shape: pl.BlockSpec(shape, lambda i, j: (0,) * len(shape))
    tile = lambda w: pl.BlockSpec((1, ts, w), lambda i, j: (i, j, 0))
    outs = [
        jax.ShapeDtypeStruct((b, t, 512), BF16),
        jax.ShapeDtypeStruct((b, t, 512), BF16),
        jax.ShapeDtypeStruct((b, t, 512), BF16),
        jax.ShapeDtypeStruct((b, t, 512), BF16),
        jax.ShapeDtypeStruct((b, t, 512), BF16),
        jax.ShapeDtypeStruct((b, t, 128), F32),
        jax.ShapeDtypeStruct((b, t, ROPE), F32),
        jax.ShapeDtypeStruct((b, POOL_MAX - 1, 256), F32),
        jax.ShapeDtypeStruct((b, CONV_W - 1, 256), F32),
    ]
    return pl.pallas_call(
        functools.partial(_prompt_in_kernel, ts=ts, nt=nt),
        out_shape=outs,
        grid=(b, nt),
        in_specs=[
            tile(d),
            pl.BlockSpec((1, 3, d), lambda i, j: (i, 0, 0)),
            const((1, d)),
            const((d, D_IN_PAD)),
            const((256, 256)), const((1, 256)), const((CONV_W, 256)),
            const((1, 256)), const((256, 512)), const((128, 128)), const((1, 128)),
            const((1, 128)), const((1, 128)),
            const((128, 512)), const((128, 128)), const((1, 128)), const((128, 512)),
            pl.BlockSpec((3, ts, 128), lambda i, j: (0, j, 0)),
        ],
        out_specs=[
            tile(512), tile(512), tile(512), tile(512), tile(512), tile(128), tile(ROPE),
            pl.BlockSpec((1, POOL_MAX - 1, 256), lambda i, j: (i, 0, 0)),
            pl.BlockSpec((1, CONV_W - 1, 256), lambda i, j: (i, 0, 0)),
        ],
        scratch_shapes=[pltpu.VMEM((HALO + ts, 256), F32), pltpu.VMEM((HALO + ts, 256), F32)],
        compiler_params=pltpu.CompilerParams(
            dimension_semantics=("parallel", "arbitrary"), vmem_limit_bytes=VMEM_LIMIT),
        name="prompt_in",
    )(x, mod_p, lw["norm_g"], lw["win"], lw["poolw"], lw["pscale"], lw["convw"],
      lw["qng"], lw["wuq"], lw["s128q"], lw["gq"], lw["kvg"], lw["krg"],
      lw["wuk"], lw["s128k"], lw["gk"], lw["wuv"], tab)


def _prompt_out_kernel(q_ref, k_ref, v_ref, gpc_ref, zmg_ref, x_ref, gate_ref, wout_ref, y_ref,
                       m_s, l_s, acc_s, *, tq):
    i = pl.program_id(1)
    m_s[...] = jnp.full(m_s.shape, NEG, F32)
    l_s[...] = jnp.zeros(l_s.shape, F32)
    acc_s[...] = jnp.zeros(acc_s.shape, F32)

    def block(j, masked):
        start = pl.multiple_of(j * tq, tq)
        for h in range(N_HEADS):
            sl = slice(h * LANES, (h + 1) * LANES)
            s = _dot_nt(q_ref[0, :, sl], k_ref[0, pl.ds(start, tq), sl])
            if masked:
                row = lax.broadcasted_iota(jnp.int32, (tq, tq), 0)
                col = lax.broadcasted_iota(jnp.int32, (tq, tq), 1)
                s = jnp.where(col <= row, s, NEG)
            m = m_s[h]
            m_new = jnp.maximum(m, jnp.max(s, axis=-1, keepdims=True))
            p = jnp.exp(s - m_new)
            corr = jnp.exp(m - m_new)
            l_s[h] = l_s[h] * corr + jnp.sum(p, axis=-1, keepdims=True)
            acc_s[h] = acc_s[h] * corr + _dot(p.astype(BF16), v_ref[0, pl.ds(start, tq), sl])
            m_s[h] = m_new

    def body(j, carry):
        block(j, False)
        return carry

    lax.fori_loop(0, i, body, 0)
    block(i, True)
    ym = jnp.concatenate([acc_s[h] / l_s[h] for h in range(N_HEADS)], axis=1) * zmg_ref[0].astype(F32)
    o = _dot(gpc_ref[0], wout_ref[0:512, :]) + _dot(ym.astype(BF16), wout_ref[512:1024, :])
    y_ref[0] = x_ref[0] + gate_ref[0, 2:3, :] * o


def _prompt_out(qp, kp, v, gpc, zmg, x, mod_p, wout, tq):
    b, t, d = x.shape
    tile = lambda w: pl.BlockSpec((1, tq, w), lambda i, j: (i, j, 0))
    full = lambda w: pl.BlockSpec((1, t, w), lambda i, j: (i, 0, 0))
    return pl.pallas_call(
        functools.partial(_prompt_out_kernel, tq=tq),
        out_shape=jax.ShapeDtypeStruct((b, t, d), F32),
        grid=(b, t // tq),
        in_specs=[
            tile(512), full(512), full(512), tile(512), tile(512), tile(d),
            pl.BlockSpec((1, 3, d), lambda i, j: (i, 0, 0)),
            pl.BlockSpec((d, d), lambda i, j: (0, 0)),
        ],
        out_specs=tile(d),
        scratch_shapes=[pltpu.VMEM((N_HEADS, tq, 1), F32), pltpu.VMEM((N_HEADS, tq, 1), F32),
                        pltpu.VMEM((N_HEADS, tq, LANES), F32)],
        compiler_params=pltpu.CompilerParams(
            dimension_semantics=("parallel", "arbitrary"), vmem_limit_bytes=VMEM_LIMIT),
        name="prompt_out",
    )(qp, kp, v, gpc, zmg, x, mod_p, wout)


def _sample_in_kernel(x_ref, mod_ref, g_ref, win_ref, poolw_ref, pscale_ref, convw_ref,
                      qng_ref, wuq_ref, s128q_ref, gq_ref, kvg_ref, krg_ref, kng_ref, wukt_ref, tab_ref,
                      pst_ref, cst_ref,
                      gpc_ref, zmg_ref, qabs_ref, qr_ref, lat_ref, krp_ref, psto_ref, csto_ref,
                      *, db, ts, d, past_len):
    shift = mod_ref[:, 0:d]
    scale1 = 1.0 + mod_ref[:, d:2 * d]
    hs = []
    for t in range(ts):
        x = x_ref[:, t * d:(t + 1) * d]
        ms = jnp.mean(x * x, axis=-1, keepdims=True)
        hs.append(((x * lax.rsqrt(ms + EPS) * g_ref[...]) * scale1 + shift).astype(BF16))
    hb = jnp.concatenate(hs, axis=0)
    rows = lambda a, t: a[t * db:(t + 1) * db]

    pp = _dot(hb, win_ref[:, C_POOL:C_POOL + 512])
    zp = pp[:, 256:512]
    np_ = POOL_MAX - 1
    ext = [pst_ref[:, k * 256:(k + 1) * 256] for k in range(np_)] + [rows(pp, t)[:, 0:256] for t in range(ts)]
    for k in range(np_):
        psto_ref[:, k * 256:(k + 1) * 256] = ext[ts + k]

    def doubled(prev, step, lo):
        return {r: prev[r] + prev[r - step] for r in range(lo, np_ + ts) if r in prev and (r - step) in prev}

    a0 = dict(enumerate(ext))
    a1 = doubled(a0, 1, 1)
    a2 = doubled(a1, 2, 3)
    a3 = doubled(a2, 4, 7)
    a4 = doubled(a3, 8, 15)
    lane = lax.broadcasted_iota(jnp.int32, (1, 256), 1)
    ds_ = []
    for t in range(ts):
        r = np_ + t
        inv = [1.0 / float(min(past_len + t + 1, w)) for w in POOL_WINDOWS]
        mean = jnp.where(lane < 64, a1[r] * inv[0],
                         jnp.where(lane < 128, a2[r] * inv[1],
                                   jnp.where(lane < 192, a3[r] * inv[2], a4[r] * inv[3])))
        ds_.append(mean - ext[r])
    dcat = jnp.concatenate(ds_, axis=0)
    gp = _dot(dcat.astype(BF16), poolw_ref[...]) * pscale_ref[...] * _silu(zp)

    pc = _dot(hb, win_ref[:, C_CONV:C_CONV + 1024])
    bg = pc[:, 256:512]
    zc = pc[:, 768:1024]
    vfull = pc[:, 512:768] * pc[:, 0:256]
    nc = CONV_W - 1
    vx = [cst_ref[:, k * 256:(k + 1) * 256] for k in range(nc)] + [rows(vfull, t) for t in range(ts)]
    for k in range(nc):
        csto_ref[:, k * 256:(k + 1) * 256] = vx[ts + k]
    ys = []
    for t in range(ts):
        acc = convw_ref[0:1, :] * vx[t]
        for k in range(1, CONV_W):
            acc = acc + convw_ref[k:k + 1, :] * vx[t + k]
        ys.append(acc)
    gc = bg * jnp.concatenate(ys, axis=0) * _silu(zc)
    gpc_ref[...] = jnp.concatenate([gp, gc], axis=1).astype(BF16)

    zmg_ref[...] = _silu(_dot(hb, win_ref[:, C_ZM:C_ZM + 512])).astype(BF16)

    pm = _dot(hb, win_ref[:, C_MLA:C_MLA + 512])
    qh, ckv, kr = _mla_prep(pm, qng_ref[...], wuq_ref[...], s128q_ref[...], gq_ref[...],
                            kvg_ref[...], krg_ref[...], tab_ref[0], tab_ref[1], tab_ref[2])
    for t in range(ts):
        lat_ref[:, t * LANES:(t + 1) * LANES] = rows(ckv, t)
        krp_ref[:, t * LANES:(t + 1) * LANES] = rows(kr, t)
    for h in range(N_HEADS):
        qa = _dot((qh[h] * kng_ref[...]).astype(BF16), wukt_ref[h])
        for t in range(ts):
            c0 = (h * ts + t) * LANES
            qabs_ref[:, c0:c0 + LANES] = rows(qa, t).astype(BF16)
            qr_ref[:, c0:c0 + LANES] = rows(qh[h], t).astype(BF16)


def _sample_in(xs2, mod_s, lw, tab, pst, cst, ts, past_len):
    db = xs2.shape[0]
    d = xs2.shape[1] // ts
    r = ts * db
    outs = [
        jax.ShapeDtypeStruct((r, 512), BF16),
        jax.ShapeDtypeStruct((r, 512), BF16),
        jax.ShapeDtypeStruct((db, N_HEADS * ts * LANES), BF16),
        jax.ShapeDtypeStruct((db, N_HEADS * ts * LANES), BF16),
        jax.ShapeDtypeStruct((db, ts * LANES), F32),
        jax.ShapeDtypeStruct((db, ts * LANES), F32),
        jax.ShapeDtypeStruct(pst.shape, F32),
        jax.ShapeDtypeStruct(cst.shape, F32),
    ]
    return pl.pallas_call(
        functools.partial(_sample_in_kernel, db=db, ts=ts, d=d, past_len=past_len),
        out_shape=outs,
        compiler_params=pltpu.CompilerParams(vmem_limit_bytes=VMEM_LIMIT),
        name="sample_in",
    )(xs2, mod_s, lw["norm_g"], lw["win"], lw["poolw"], lw["pscale"], lw["convw"],
      lw["qng"], lw["wuq"], lw["s128q"], lw["gq"], lw["kvg"], lw["krg"], lw["kngpad"], lw["wukt_pad"], tab,
      pst, cst)


def _sample_attn_kernel(pt_ref, qabs_ref, qr_ref, latn_ref, krn_ref, wukt_ref, lat_hbm, krt_hbm, out_ref,
                        lhs, newlat, newkr, lat_buf, krt_buf, latb, sem, m_s, l_s, acc_s,
                        *, layer, g, ts, page, n_pages):
    b = pl.program_id(0)
    nb = pl.num_programs(0)
    nc = n_pages // g
    nq = N_HEADS * ts
    nk = N_HEADS * NOPE

    def page_copies(seq, chunk, slot, j):
        pg = pt_ref[seq * n_pages + chunk * g + j]
        return (pltpu.make_async_copy(lat_hbm.at[layer, pg], lat_buf.at[slot, j], sem.at[0, slot]),
                pltpu.make_async_copy(krt_hbm.at[layer, pg], krt_buf.at[slot, j], sem.at[1, slot]))

    def start_chunk(seq, chunk, slot):
        for j in range(g):
            for cp in page_copies(seq, chunk, slot, j):
                cp.start()

    def wait_chunk(seq, chunk, slot):
        for j in range(g):
            for cp in page_copies(seq, chunk, slot, j):
                cp.wait()

    def scores(lat_bf, s_rope):
        res = _dot_nt(lhs[...], lat_bf)
        kt = res[0:nk]
        k2 = kt * kt
        rr = []
        for h in range(N_HEADS):
            ss = jnp.sum(k2[h * NOPE:(h + 1) * NOPE], axis=0, keepdims=True)
            rr.append(jnp.broadcast_to(lax.rsqrt(ss * (1.0 / NOPE) + EPS), (ts, page)))
        return res[nk:nk + nq] * jnp.concatenate(rr, axis=0) + s_rope

    def softmax_update(s, lat_bf_all):
        m_old = m_s[:, 0:1]
        m_new = jnp.maximum(m_old, jnp.max(s, axis=-1, keepdims=True))
        p = jnp.exp(s - m_new)
        corr = jnp.exp(m_old - m_new)
        l_s[...] = jnp.broadcast_to(l_s[:, 0:1] * corr + jnp.sum(p, axis=-1, keepdims=True), (nq, LANES))
        acc_s[...] = acc_s[...] * corr + _dot(p.astype(BF16), lat_bf_all)
        m_s[...] = jnp.broadcast_to(m_new, (nq, LANES))

    @pl.when(b == 0)
    def _():
        start_chunk(0, 0, 0)
        lhs[0:nk, :] = wukt_ref[...]
        newlat[...] = jnp.zeros((page, LANES), F32)
        newkr[...] = jnp.zeros((page, ROPE), F32)

    qr_bf = qr_ref[0].astype(F32)[:, NOPE:NOPE + ROPE].astype(BF16)
    lhs[nk:nk + nq, :] = qabs_ref[0]
    newlat[0:ts, :] = latn_ref[0]
    newkr[0:ts, :] = krn_ref[0][:, NOPE:NOPE + ROPE]
    m_s[...] = jnp.full((nq, LANES), NEG, F32)
    l_s[...] = jnp.zeros((nq, LANES), F32)
    acc_s[...] = jnp.zeros((nq, LANES), F32)
    lat_bf = newlat[...].astype(BF16)
    s = scores(lat_bf, _dot_nt(qr_bf, newkr[...].astype(BF16)))
    key = lax.broadcasted_iota(jnp.int32, (nq, page), 1)
    qi = lax.broadcasted_iota(jnp.int32, (nq, page), 0) % ts
    softmax_update(jnp.where(key <= qi, s, NEG), lat_bf)

    def compute(slot):
        parts = []
        for j in range(g):
            lat_bf = lat_buf[slot, j].astype(BF16)
            latb[j * page:(j + 1) * page, :] = lat_bf
            parts.append(scores(lat_bf, _dot(qr_bf, krt_buf[slot, j].astype(BF16))))
        softmax_update(jnp.concatenate(parts, axis=1), latb[...])

    def pair(cc, carry):
        c0 = cc * 2
        start_chunk(b, c0 + 1, 1)
        wait_chunk(b, c0, 0)
        compute(0)

        @pl.when(c0 + 2 < nc)
        def _():
            start_chunk(b, c0 + 2, 0)

        @pl.when(jnp.logical_and(c0 + 2 == nc, b + 1 < nb))
        def _():
            start_chunk(b + 1, 0, 0)

        wait_chunk(b, c0 + 1, 1)
        compute(1)
        return carry

    lax.fori_loop(0, nc // 2, pair, 0)
    out_ref[0] = acc_s[...] / l_s[...]


def _sample_attn(page_table, qabs, qr, latn, krn, wukt, cache_latent, cache_krope_t, layer, g, ts):
    db, n_pages = page_table.shape
    page = cache_latent.shape[2]
    assert n_pages % (2 * g) == 0
    nq = N_HEADS * ts
    per_seq = lambda rows, w: pl.BlockSpec((1, rows, w), lambda b, pt: (b, 0, 0))
    grid_spec = pltpu.PrefetchScalarGridSpec(
        num_scalar_prefetch=1,
        grid=(db,),
        in_specs=[per_seq(nq, LANES), per_seq(nq, LANES), per_seq(ts, LANES), per_seq(ts, LANES),
                  pl.BlockSpec((N_HEADS * NOPE, LANES), lambda b, pt: (0, 0)),
                  pl.BlockSpec(memory_space=pl.ANY), pl.BlockSpec(memory_space=pl.ANY)],
        out_specs=per_seq(nq, LANES),
        scratch_shapes=[
            pltpu.VMEM((N_HEADS * NOPE + nq, LANES), BF16),
            pltpu.VMEM((page, LANES), F32), pltpu.VMEM((page, ROPE), F32),
            pltpu.VMEM((2, g, page, LANES), F32),
            pltpu.VMEM((2, g, ROPE, page), F32),
            pltpu.VMEM((g * page, LANES), BF16),
            pltpu.SemaphoreType.DMA((2, 2)),
            pltpu.VMEM((nq, LANES), F32), pltpu.VMEM((nq, LANES), F32), pltpu.VMEM((nq, LANES), F32),
        ],
    )
    return pl.pallas_call(
        functools.partial(_sample_attn_kernel, layer=layer, g=g, ts=ts, page=page, n_pages=n_pages),
        out_shape=jax.ShapeDtypeStruct((db, nq, LANES), F32),
        grid_spec=grid_spec,
        compiler_params=pltpu.CompilerParams(
            dimension_semantics=("arbitrary",), vmem_limit_bytes=VMEM_LIMIT),
        name="sample_attn",
    )(page_table.reshape(-1), qabs, qr, latn, krn, wukt, cache_latent, cache_krope_t)


def _sample_out_kernel(lo_ref, wuv_ref, gpc_ref, zmg_ref, x_ref, mod_ref, wout_ref, y_ref, *, db, ts, d):
    gate = mod_ref[:, 2 * d:3 * d]
    for t in range(ts):
        ym = jnp.concatenate(
            [_dot(lo_ref[:, (h * ts + t) * LANES:(h * ts + t + 1) * LANES].astype(BF16),
                  wuv_ref[:, h * LANES:(h + 1) * LANES]) for h in range(N_HEADS)], axis=1)
        ymg = (ym * zmg_ref[t * db:(t + 1) * db, :].astype(F32)).astype(BF16)
        o = _dot(gpc_ref[t * db:(t + 1) * db, :], wout_ref[0:512, :]) + _dot(ymg, wout_ref[512:1024, :])
        y_ref[:, t * d:(t + 1) * d] = x_ref[:, t * d:(t + 1) * d] + gate * o


def _sample_out(lat_out2, wuv, gpc, zmg, xs2, mod_s, wout, ts):
    db = xs2.shape[0]
    d = xs2.shape[1] // ts
    return pl.pallas_call(
        functools.partial(_sample_out_kernel, db=db, ts=ts, d=d),
        out_shape=jax.ShapeDtypeStruct(xs2.shape, F32),
        compiler_params=pltpu.CompilerParams(vmem_limit_bytes=VMEM_LIMIT),
        name="sample_out",
    )(lat_out2, wuv, gpc, zmg, xs2, mod_s, wout)


def _rope_tables(pos):
    inv_freq = 1.0 / (ROPE_THETA ** (jnp.arange(0, ROPE, 2, dtype=F32) / ROPE))
    ang = pos.astype(F32)[:, None] * inv_freq[None, :]
    cos, sin = jnp.cos(ang), jnp.sin(ang)
    n = pos.shape[0]
    half = ROPE // 2
    z = lambda w: jnp.zeros((n, w), F32)
    cos_t = jnp.concatenate([jnp.ones((n, NOPE), F32), cos, cos, jnp.ones((n, LANES - NOPE - ROPE), F32)], axis=1)
    sin_a = jnp.concatenate([z(NOPE), -sin, z(LANES - NOPE - half)], axis=1)
    sin_b = jnp.concatenate([z(NOPE + half), sin, z(LANES - NOPE - ROPE)], axis=1)
    return jnp.stack([cos_t, sin_a, sin_b])


def _layer_weights(l, norm_g, w_in, pool_w, pool_scale, conv_w, q_norm_g, w_uq, qn_g, qr_g,
                   kv_norm_g, kr_g, w_uk, kn_g, w_uv, w_out):
    d = w_in.shape[1]
    wi = w_in[l]
    zc = lambda w: jnp.zeros((d, w), F32)
    win = jnp.concatenate([wi[:, 0:1920], zc(NOPE), wi[:, 1920:1952], zc(LANES - NOPE - ROPE), wi[:, 1952:]], axis=1)
    pad_head = lambda w, per: jnp.concatenate(
        [jnp.concatenate([w[:, h * per:(h + 1) * per], jnp.zeros((w.shape[0], LANES - per), F32)], axis=1)
         for h in range(N_HEADS)], axis=1)
    seg = jnp.arange(LANES)
    nope_m = (seg < NOPE).astype(F32)
    rope_m = ((seg >= NOPE) & (seg < NOPE + ROPE)).astype(F32)
    s128k = nope_m[:, None] * nope_m[None, :] / NOPE
    s128q = s128k + rope_m[:, None] * rope_m[None, :] / ROPE
    zl = lambda w: jnp.zeros((w,), F32)
    pw = pool_w[l]
    gw = pw.shape[1]
    poolw = jnp.zeros((256, 256), F32)
    for gi in range(pw.shape[0]):
        poolw = poolw.at[gi * gw:(gi + 1) * gw, gi * gw:(gi + 1) * gw].set(pw[gi])
    wukt_pad = jnp.stack([jnp.concatenate([w_uk[l][:, h * NOPE:(h + 1) * NOPE].T,
                                           jnp.zeros((LANES - NOPE, w_uk.shape[1]), F32)], axis=0)
                          for h in range(N_HEADS)])
    return {
        "norm_g": norm_g[l][None, :],
        "win": win.astype(BF16),
        "poolw": poolw.astype(BF16),
        "pscale": pool_scale[l][None, :],
        "convw": conv_w[l],
        "qng": q_norm_g[l][None, :],
        "wuq": pad_head(w_uq[l], NOPE + ROPE).astype(BF16),
        "s128q": s128q.astype(BF16),
        "s128k": s128k.astype(BF16),
        "gq": (jnp.concatenate([qn_g[l], qr_g[l], zl(LANES - NOPE - ROPE)]) * SM_SCALE)[None, :],
        "kvg": kv_norm_g[l][None, :],
        "krg": jnp.concatenate([zl(NOPE), kr_g[l], zl(LANES - NOPE - ROPE)])[None, :],
        "wuk": pad_head(w_uk[l], NOPE).astype(BF16),
        "gk": jnp.concatenate([kn_g[l], zl(LANES - NOPE)])[None, :],
        "kngpad": jnp.concatenate([kn_g[l], zl(LANES - NOPE)])[None, :],
        "wuv": w_uv[l].astype(BF16),
        "wukt": w_uk[l].T.astype(BF16),
        "wukt_pad": wukt_pad.astype(BF16),
        "wout": w_out[l].astype(BF16),
    }


def kernel(x_prompt, x_sample, cache_latent, cache_krope, state_pool, state_conv, page_table,
           c_prompt, c_sample, norm_g, w_ada, b_ada, w_in, pool_w, pool_scale, conv_w,
           q_norm_g, w_uq, qn_g, qr_g, kv_norm_g, kr_g, w_uk, kn_g, w_uv, w_out):
    depth = w_in.shape[0]
    bp, seq, d = x_prompt.shape
    db, ts, _ = x_sample.shape
    page = cache_latent.shape[2]
    past_len = page_table.shape[1] * page
    ts_p = 512 if seq % 512 == 0 else seq
    tq = 512 if seq % 512 == 0 else seq
    g_pages = 16
    cache_krope_t = jnp.swapaxes(cache_krope, 2, 3)

    mod = _modulation(jnp.concatenate([c_prompt, c_sample], axis=0), w_ada, b_ada)
    tab_p = _rope_tables(jnp.arange(seq, dtype=jnp.int32))
    tab_s = jnp.repeat(_rope_tables(past_len + jnp.arange(ts, dtype=jnp.int32)), db, axis=1)
    lws = [_layer_weights(l, norm_g, w_in, pool_w, pool_scale, conv_w, q_norm_g, w_uq, qn_g, qr_g,
                          kv_norm_g, kr_g, w_uk, kn_g, w_uv, w_out) for l in range(depth)]

    xp = x_prompt
    lat_p, kr_p, pool_p, conv_p = [], [], [], []
    for l in range(depth):
        mod_p = mod[l, :bp].reshape(bp, 3, d)
        gpc, zmg, qp, kp, v, lat, kr, pst, cst = _prompt_in(xp, mod_p, lws[l], tab_p, ts_p)
        xp = _prompt_out(qp, kp, v, gpc, zmg, xp, mod_p, lws[l]["wout"], tq)
        lat_p.append(lat)
        kr_p.append(kr)
        pool_p.append(pst)
        conv_p.append(cst)

    xs2 = x_sample.reshape(db, ts * d)
    lat_s, kr_s, pool_s, conv_s = [], [], [], []
    for l in range(depth):
        mod_s = mod[l, bp:]
        pst = state_pool[l].reshape(db, -1)
        cst = state_conv[l].reshape(db, -1)
        gpc, zmg, qabs, qr, latn, krn, psto, csto = _sample_in(xs2, mod_s, lws[l], tab_s, pst, cst, ts, past_len)
        nq = N_HEADS * ts
        lat_out = _sample_attn(page_table, qabs.reshape(db, nq, LANES), qr.reshape(db, nq, LANES),
                               latn.reshape(db, ts, LANES), krn.reshape(db, ts, LANES), lws[l]["wukt"],
                               cache_latent, cache_krope_t, l, g_pages, ts)
        xs2 = _sample_out(lat_out.reshape(db, nq * LANES), lws[l]["wuv"], gpc, zmg, xs2, mod_s,
                          lws[l]["wout"], ts)
        lat_s.append(latn.reshape(db, ts, LANES))
        kr_s.append(krn.reshape(db, ts, LANES)[:, :, NOPE:NOPE + ROPE])
        pool_s.append(psto.reshape(state_pool.shape[1:]))
        conv_s.append(csto.reshape(state_conv.shape[1:]))

    return (xp, xs2.reshape(x_sample.shape),
            jnp.stack(lat_p), jnp.stack(kr_p), jnp.stack(pool_p), jnp.stack(conv_p),
            jnp.stack(lat_s), jnp.stack(kr_s), jnp.stack(pool_s), jnp.stack(conv_s))
```

```python
import functools
import math

import jax
import jax.numpy as jnp
from jax import lax
from jax.experimental import pallas as pl
from jax.experimental.pallas import tpu as pltpu

F32 = jnp.float32
BF16 = jnp.bfloat16

POOL_WINDOWS = (2, 4, 8, 16)
POOL_MAX = max(POOL_WINDOWS)
CONV_W = 3
N_HEADS = 4
NOPE = 64
ROPE = 32
ROPE_THETA = 10000.0
EPS = 1e-6
NEG = -1e30
SM_SCALE = 1.0 / math.sqrt(NOPE + ROPE)
LANES = 128
HALO = 16
VMEM_LIMIT = 56 * 1024 * 1024

C_POOL = 0
C_CONV = 512
C_MLA = 1536
C_ZM = 2048
D_IN_PAD = 2560


def _sigmoid(x):
    return 1.0 / (1.0 + jnp.exp(-x))


def _silu(x):
    return x * _sigmoid(x)


def _dot(a, b):
    return jnp.dot(a, b, preferred_element_type=F32)


def _dot_nt(a, b):
    return lax.dot_general(a, b, (((1,), (1,)), ((), ())), preferred_element_type=F32)


def _rms(x, n):
    return x * lax.rsqrt(jnp.sum(x * x, axis=-1, keepdims=True) * (1.0 / n) + EPS)


def _rope_block(x, cos_t, sin_a, sin_b):
    return x * cos_t + pltpu.roll(x, LANES - ROPE // 2, 1) * sin_a + pltpu.roll(x, ROPE // 2, 1) * sin_b


def _mod_kernel(c_ref, w_ref, b_ref, o_ref):
    a = _silu(c_ref[...]).astype(BF16)
    o_ref[0] = _dot(a, w_ref[0].astype(BF16)) + b_ref[0]


def _modulation(c_all, w_ada, b_ada):
    depth, d, d3 = w_ada.shape
    r = c_all.shape[0]
    nb = d3 // d
    return pl.pallas_call(
        _mod_kernel,
        out_shape=jax.ShapeDtypeStruct((depth, r, d3), F32),
        grid=(depth, nb),
        in_specs=[
            pl.BlockSpec((r, d), lambda l, j: (0, 0)),
            pl.BlockSpec((1, d, d), lambda l, j: (l, 0, j)),
            pl.BlockSpec((1, 1, d), lambda l, j: (l, 0, j)),
        ],
        out_specs=pl.BlockSpec((1, r, d), lambda l, j: (l, 0, j)),
        compiler_params=pltpu.CompilerParams(vmem_limit_bytes=VMEM_LIMIT),
        name="adaln_mod",
    )(c_all, w_ada, b_ada.reshape(depth, 1, d3))


def _mla_prep(pm, qng, wuq, s128q, gq, kvg, krg, cos_t, sin_a, sin_b):
    cq = pm[:, 0:256]
    ckv_raw = pm[:, 256:384]
    kr_raw = pm[:, 384:512]
    cqn = (_rms(cq, 256) * qng).astype(BF16)
    q = _dot(cqn, wuq)
    qh = []
    for h in range(N_HEADS):
        xh = q[:, h * LANES:(h + 1) * LANES]
        ms = _dot((xh * xh).astype(BF16), s128q)
        xh = xh * lax.rsqrt(ms + EPS) * gq
        qh.append(_rope_block(xh, cos_t, sin_a, sin_b))
    ckv = _rms(ckv_raw, 128) * kvg
    krn = _rms(kr_raw, ROPE) * krg
    kr = _rope_block(krn, cos_t, sin_a, sin_b)
    return qh, ckv, kr


def _prompt_in_kernel(x_ref, mod_ref, g_ref, win_ref, poolw_ref, pscale_ref, convw_ref,
                      qng_ref, wuq_ref, s128q_ref, gq_ref, kvg_ref, krg_ref,
                      wuk_ref, s128k_ref, gk_ref, wuv_ref, tab_ref,
                      gpc_ref, zmg_ref, qp_ref, kp_ref, v_ref, lat_ref, kr_ref, pst_ref, cst_ref,
                      uext, vext, *, ts, nt):
    t = pl.program_id(1)
    x = x_ref[0]
    ms = jnp.mean(x * x, axis=-1, keepdims=True)
    h = (x * lax.rsqrt(ms + EPS) * g_ref[...]) * (1.0 + mod_ref[0, 1:2, :]) + mod_ref[0, 0:1, :]
    hb = h.astype(BF16)

    @pl.when(t == 0)
    def _():
        uext[0:HALO, :] = jnp.zeros((HALO, 256), F32)
        vext[0:HALO, :] = jnp.zeros((HALO, 256), F32)

    pp = _dot(hb, win_ref[:, C_POOL:C_POOL + 512])
    u = pp[:, 0:256]
    zp = pp[:, 256:512]
    uext[HALO:HALO + ts, :] = u
    ext = uext[...]
    a1 = ext + pltpu.roll(ext, 1, 0)
    a2 = a1 + pltpu.roll(a1, 2, 0)
    a3 = a2[:, 128:256]
    a3 = a3 + pltpu.roll(a3, 4, 0)
    a4 = a3 + pltpu.roll(a3, 8, 0)
    pos1 = t * ts + lax.broadcasted_iota(jnp.int32, (ts, 1), 0) + 1
    inv = [1.0 / jnp.minimum(pos1, w).astype(F32) for w in POOL_WINDOWS]
    lane = lax.broadcasted_iota(jnp.int32, (1, LANES), 1)
    mean_l = jnp.where(lane < 64, a1[HALO:, 0:128] * inv[0], a2[HALO:, 0:128] * inv[1])
    mean_r = jnp.where(lane < 64, a3[HALO:] * inv[2], a4[HALO:] * inv[3])
    d = jnp.concatenate([mean_l, mean_r], axis=1) - u
    yp = _dot(d.astype(BF16), poolw_ref[...]) * pscale_ref[...]
    gp = yp * _silu(zp)

    @pl.when(t == nt - 1)
    def _():
        pst_ref[0] = uext[pl.ds(HALO + ts - (POOL_MAX - 1), POOL_MAX - 1), :]

    uext[0:HALO, :] = u[ts - HALO:ts, :]

    pc = _dot(hb, win_ref[:, C_CONV:C_CONV + 1024])
    hc = pc[:, 0:256]
    bg = pc[:, 256:512]
    cg = pc[:, 512:768]
    zc = pc[:, 768:1024]
    v = cg * hc
    vext[HALO:HALO + ts, :] = v
    e = vext[...]
    y = convw_ref[0:1, :] * pltpu.roll(e, 2, 0) + convw_ref[1:2, :] * pltpu.roll(e, 1, 0) + convw_ref[2:3, :] * e
    gc = bg * y[HALO:] * _silu(zc)
    gpc_ref[0] = jnp.concatenate([gp, gc], axis=1).astype(BF16)

    @pl.when(t == nt - 1)
    def _():
        cst_ref[0] = vext[pl.ds(HALO + ts - (CONV_W - 1), CONV_W - 1), :]

    vext[0:HALO, :] = v[ts - HALO:ts, :]

    zmg_ref[0] = _silu(_dot(hb, win_ref[:, C_ZM:C_ZM + 512])).astype(BF16)

    pm = _dot(hb, win_ref[:, C_MLA:C_MLA + 512])
    cos_t = tab_ref[0]
    sin_a = tab_ref[1]
    sin_b = tab_ref[2]
    qh, ckv, kr = _mla_prep(pm, qng_ref[...], wuq_ref[...], s128q_ref[...], gq_ref[...],
                            kvg_ref[...], krg_ref[...], cos_t, sin_a, sin_b)
    qp_ref[0] = jnp.concatenate(qh, axis=1).astype(BF16)
    lat_ref[0] = ckv
    kr_ref[0] = kr[:, NOPE:NOPE + ROPE]
    cb = ckv.astype(BF16)
    kc = _dot(cb, wuk_ref[...])
    kh = []
    for hh in range(N_HEADS):
        xh = kc[:, hh * LANES:(hh + 1) * LANES]
        msk = _dot((xh * xh).astype(BF16), s128k_ref[...])
        kh.append(xh * lax.rsqrt(msk + EPS) * gk_ref[...] + kr)
    kp_ref[0] = jnp.concatenate(kh, axis=1).astype(BF16)
    v_ref[0] = _dot(cb, wuv_ref[...]).astype(BF16)


def _prompt_in(x, mod_p, lw, tab, ts):
    b, t, d = x.shape
    nt = t // ts
    const = lambda shape: pl.BlockSpec(shape, lambda i, j: (0,) * len(shape))
    tile = lambda w: pl.BlockSpec((1, ts, w), lambda i, j: (i, j, 0))
    outs = [
        jax.ShapeDtypeStruct((b, t, 512), BF16),
        jax.ShapeDtypeStruct((b, t, 512), BF16),
        jax.ShapeDtypeStruct((b, t, 512), BF16),
        jax.ShapeDtypeStruct((b, t, 512), BF16),
        jax.ShapeDtypeStruct((b, t, 512), BF16),
        jax.ShapeDtypeStruct((b, t, 128), F32),
        jax.ShapeDtypeStruct((b, t, ROPE), F32),
        jax.ShapeDtypeStruct((b, POOL_MAX - 1, 256), F32),
        jax.ShapeDtypeStruct((b, CONV_W - 1, 256), F32),
    ]
    return pl.pallas_call(
        functools.partial(_prompt_in_kernel, ts=ts, nt=nt),
        out_shape=outs,
        grid=(b, nt),
        in_specs=[
            tile(d),
            pl.BlockSpec((1, 3, d), lambda i, j: (i, 0, 0)),
            const((1, d)),
            const((d, D_IN_PAD)),
            const((256, 256)), const((1, 256)), const((CONV_W, 256)),
            const((1, 256)), const((256, 512)), const((128, 128)), const((1, 128)),
            const((1, 128)), const((1, 128)),
            const((128, 512)), const((128, 128)), const((1, 128)), const((128, 512)),
            pl.BlockSpec((3, ts, 128), lambda i, j: (0, j, 0)),
        ],
        out_specs=[
            tile(512), tile(512), tile(512), tile(512), tile(512), tile(128), tile(ROPE),
            pl.BlockSpec((1, POOL_MAX - 1, 256), lambda i, j: (i, 0, 0)),
            pl.BlockSpec((1, CONV_W - 1, 256), lambda i, j: (i, 0, 0)),
        ],
        scratch_shapes=[pltpu.VMEM((HALO + ts, 256), F32), pltpu.VMEM((HALO + ts, 256), F32)],
        compiler_params=pltpu.CompilerParams(
            dimension_semantics=("parallel", "arbitrary"), vmem_limit_bytes=VMEM_LIMIT),
        name="prompt_in",
    )(x, mod_p, lw["norm_g"], lw["win"], lw["poolw"], lw["pscale"], lw["convw"],
      lw["qng"], lw["wuq"], lw["s128q"], lw["gq"], lw["kvg"], lw["krg"],
      lw["wuk"], lw["s128k"], lw["gk"], lw["wuv"], tab)


def _prompt_out_kernel(q_ref, k_ref, v_ref, gpc_ref, zmg_ref, x_ref, gate_ref, wout_ref, y_ref,
                       m_s, l_s, acc_s, *, tq):
    i = pl.program_id(1)
    m_s[...] = jnp.full(m_s.shape, NEG, F32)
    l_s[...] = jnp.zeros(l_s.shape, F32)
    acc_s[...] = jnp.zeros(acc_s.shape, F32)

    def block(j, masked):
        start = pl.multiple_of(j * tq, tq)
        for h in range(N_HEADS):
            sl = slice(h * LANES, (h + 1) * LANES)
            s = _dot_nt(q_ref[0, :, sl], k_ref[0, pl.ds(start, tq), sl])
            if masked:
                row = lax.broadcasted_iota(jnp.int32, (tq, tq), 0)
                col = lax.broadcasted_iota(jnp.int32, (tq, tq), 1)
                s = jnp.where(col <= row, s, NEG)
            m = m_s[h]
            m_new = jnp.maximum(m, jnp.max(s, axis=-1, keepdims=True))
            p = jnp.exp(s - m_new)
            corr = jnp.exp(m - m_new)
            psum = p[:, 0:LANES]
            for c in range(1, tq // LANES):
                psum = psum + p[:, c * LANES:(c + 1) * LANES]
            l_s[h] = l_s[h] * corr + psum
            acc_s[h] = acc_s[h] * corr + _dot(p.astype(BF16), v_ref[0, pl.ds(start, tq), sl])
            m_s[h] = m_new

    def body(j, carry):
        block(j, False)
        return carry

    lax.fori_loop(0, i, body, 0)
    block(i, True)
    ym = jnp.concatenate([acc_s[h] / jnp.sum(l_s[h], axis=-1, keepdims=True) for h in range(N_HEADS)],
                         axis=1) * zmg_ref[0].astype(F32)
    o = _dot(gpc_ref[0], wout_ref[0:512, :]) + _dot(ym.astype(BF16), wout_ref[512:1024, :])
    y_ref[0] = x_ref[0] + gate_ref[0, 2:3, :] * o


def _prompt_out(qp, kp, v, gpc, zmg, x, mod_p, wout, tq):
    b, t, d = x.shape
    tile = lambda w: pl.BlockSpec((1, tq, w), lambda i, j: (i, j, 0))
    full = lambda w: pl.BlockSpec((1, t, w), lambda i, j: (i, 0, 0))
    return pl.pallas_call(
        functools.partial(_prompt_out_kernel, tq=tq),
        out_shape=jax.ShapeDtypeStruct((b, t, d), F32),
        grid=(b, t // tq),
        in_specs=[
            tile(512), full(512), full(512), tile(512), tile(512), tile(d),
            pl.BlockSpec((1, 3, d), lambda i, j: (i, 0, 0)),
            pl.BlockSpec((d, d), lambda i, j: (0, 0)),
        ],
        out_specs=tile(d),
        scratch_shapes=[pltpu.VMEM((N_HEADS, tq, 1), F32), pltpu.VMEM((N_HEADS, tq, LANES), F32),
                        pltpu.VMEM((N_HEADS, tq, LANES), F32)],
        compiler_params=pltpu.CompilerParams(
            dimension_semantics=("parallel", "arbitrary"), vmem_limit_bytes=VMEM_LIMIT),
        name="prompt_out",
    )(qp, kp, v, gpc, zmg, x, mod_p, wout)


def _sample_in_kernel(x_ref, mod_ref, g_ref, win_ref, poolw_ref, pscale_ref, convw_ref,
                      qng_ref, wuq_ref, s128q_ref, gq_ref, kvg_ref, krg_ref, kng_ref, wukt_ref, tab_ref,
                      pst_ref, cst_ref,
                      gpc_ref, zmg_ref, qabs_ref, qr_ref, lat_ref, krp_ref, psto_ref, csto_ref,
                      *, db, ts, d, past_len):
    shift = mod_ref[:, 0:d]
    scale1 = 1.0 + mod_ref[:, d:2 * d]
    hs = []
    for t in range(ts):
        x = x_ref[:, t * d:(t + 1) * d]
        ms = jnp.mean(x * x, axis=-1, keepdims=True)
        hs.append(((x * lax.rsqrt(ms + EPS) * g_ref[...]) * scale1 + shift).astype(BF16))
    hb = jnp.concatenate(hs, axis=0)
    rows = lambda a, t: a[t * db:(t + 1) * db]

    pp = _dot(hb, win_ref[:, C_POOL:C_POOL + 512])
    zp = pp[:, 256:512]
    np_ = POOL_MAX - 1
    ext = [pst_ref[:, k * 256:(k + 1) * 256] for k in range(np_)] + [rows(pp, t)[:, 0:256] for t in range(ts)]
    for k in range(np_):
        psto_ref[:, k * 256:(k + 1) * 256] = ext[ts + k]

    def doubled(prev, step, lo):
        return {r: prev[r] + prev[r - step] for r in range(lo, np_ + ts) if r in prev and (r - step) in prev}

    a0 = dict(enumerate(ext))
    a1 = doubled(a0, 1, 1)
    a2 = doubled(a1, 2, 3)
    a3 = doubled(a2, 4, 7)
    a4 = doubled(a3, 8, 15)
    lane = lax.broadcasted_iota(jnp.int32, (1, 256), 1)
    ds_ = []
    for t in range(ts):
        r = np_ + t
        inv = [1.0 / float(min(past_len + t + 1, w)) for w in POOL_WINDOWS]
        mean = jnp.where(lane < 64, a1[r] * inv[0],
                         jnp.where(lane < 128, a2[r] * inv[1],
                                   jnp.where(lane < 192, a3[r] * inv[2], a4[r] * inv[3])))
        ds_.append(mean - ext[r])
    dcat = jnp.concatenate(ds_, axis=0)
    gp = _dot(dcat.astype(BF16), poolw_ref[...]) * pscale_ref[...] * _silu(zp)

    pc = _dot(hb, win_ref[:, C_CONV:C_CONV + 1024])
    bg = pc[:, 256:512]
    zc = pc[:, 768:1024]
    vfull = pc[:, 512:768] * pc[:, 0:256]
    nc = CONV_W - 1
    vx = [cst_ref[:, k * 256:(k + 1) * 256] for k in range(nc)] + [rows(vfull, t) for t in range(ts)]
    for k in range(nc):
        csto_ref[:, k * 256:(k + 1) * 256] = vx[ts + k]
    ys = []
    for t in range(ts):
        acc = convw_ref[0:1, :] * vx[t]
        for k in range(1, CONV_W):
            acc = acc + convw_ref[k:k + 1, :] * vx[t + k]
        ys.append(acc)
    gc = bg * jnp.concatenate(ys, axis=0) * _silu(zc)
    gpc_ref[...] = jnp.concatenate([gp, gc], axis=1).astype(BF16)

    zmg_ref[...] = _silu(_dot(hb, win_ref[:, C_ZM:C_ZM + 512])).astype(BF16)

    pm = _dot(hb, win_ref[:, C_MLA:C_MLA + 512])
    qh, ckv, kr = _mla_prep(pm, qng_ref[...], wuq_ref[...], s128q_ref[...], gq_ref[...],
                            kvg_ref[...], krg_ref[...], tab_ref[0], tab_ref[1], tab_ref[2])
    for t in range(ts):
        lat_ref[:, t * LANES:(t + 1) * LANES] = rows(ckv, t)
        krp_ref[:, t * LANES:(t + 1) * LANES] = rows(kr, t)
    for h in range(N_HEADS):
        qa = _dot((qh[h] * kng_ref[...]).astype(BF16), wukt_ref[h])
        for t in range(ts):
            c0 = (h * ts + t) * LANES
            qabs_ref[:, c0:c0 + LANES] = rows(qa, t).astype(BF16)
            qr_ref[:, c0:c0 + LANES] = rows(qh[h], t).astype(BF16)


def _sample_in(xs2, mod_s, lw, tab, pst, cst, ts, past_len):
    db = xs2.shape[0]
    d = xs2.shape[1] // ts
    r = ts * db
    outs = [
        jax.ShapeDtypeStruct((r, 512), BF16),
        jax.ShapeDtypeStruct((r, 512), BF16),
        jax.ShapeDtypeStruct((db, N_HEADS * ts * LANES), BF16),
        jax.ShapeDtypeStruct((db, N_HEADS * ts * LANES), BF16),
        jax.ShapeDtypeStruct((db, ts * LANES), F32),
        jax.ShapeDtypeStruct((db, ts * LANES), F32),
        jax.ShapeDtypeStruct(pst.shape, F32),
        jax.ShapeDtypeStruct(cst.shape, F32),
    ]
    return pl.pallas_call(
        functools.partial(_sample_in_kernel, db=db, ts=ts, d=d, past_len=past_len),
        out_shape=outs,
        compiler_params=pltpu.CompilerParams(vmem_limit_bytes=VMEM_LIMIT),
        name="sample_in",
    )(xs2, mod_s, lw["norm_g"], lw["win"], lw["poolw"], lw["pscale"], lw["convw"],
      lw["qng"], lw["wuq"], lw["s128q"], lw["gq"], lw["kvg"], lw["krg"], lw["kngpad"], lw["wukt_pad"], tab,
      pst, cst)


def _sample_attn_kernel(pt_ref, qabs_ref, qr_ref, latn_ref, krn_ref, wukt_ref, lat_hbm, krt_hbm, out_ref,
                        lhs, newlat, newkr, lat_buf, krt_buf, latb, s_scr, mx_scr, sem, m_s, l_s, acc_s,
                        *, layer, g, ts, page, n_pages):
    b = pl.program_id(0)
    nb = pl.num_programs(0)
    nc = n_pages // g
    nq = N_HEADS * ts
    nk = N_HEADS * NOPE

    def page_copies(seq, chunk, slot, j):
        pg = pt_ref[seq * n_pages + chunk * g + j]
        return (pltpu.make_async_copy(lat_hbm.at[layer, pg], lat_buf.at[slot, j], sem.at[0, slot]),
                pltpu.make_async_copy(krt_hbm.at[layer, pg], krt_buf.at[slot, j], sem.at[1, slot]))

    def start_chunk(seq, chunk, slot):
        for j in range(g):
            for cp in page_copies(seq, chunk, slot, j):
                cp.start()

    def wait_chunk(slot):
        pltpu.make_async_copy(lat_hbm.at[layer, pl.ds(0, g)], lat_buf.at[slot], sem.at[0, slot]).wait()
        pltpu.make_async_copy(krt_hbm.at[layer, pl.ds(0, g)], krt_buf.at[slot], sem.at[1, slot]).wait()

    def scores(lat_bf, s_rope):
        res = _dot_nt(lhs[...], lat_bf)
        kt = res[0:nk]
        k2 = kt * kt
        rr = []
        for h in range(N_HEADS):
            ss = jnp.sum(k2[h * NOPE:(h + 1) * NOPE], axis=0, keepdims=True)
            rr.append(jnp.broadcast_to(lax.rsqrt(ss * (1.0 / NOPE) + EPS), (ts, lat_bf.shape[0])))
        return res[nk:nk + nq] * jnp.concatenate(rr, axis=0) + s_rope

    def tail(slot):
        m_old = m_s[...]
        m_new = jnp.maximum(m_old, mx_scr[slot])
        corr = jnp.exp(m_old - m_new)
        psum = jnp.zeros((nq, LANES), F32)
        ps = []
        for t in range(g * page // LANES):
            p_t = jnp.exp(s_scr[slot, :, t * LANES:(t + 1) * LANES] - m_new)
            psum = psum + p_t
            ps.append(p_t.astype(BF16))
        l_s[...] = l_s[...] * corr + psum
        acc_s[...] = acc_s[...] * corr + _dot(jnp.concatenate(ps, axis=1), latb[slot])
        m_s[...] = m_new

    @pl.when(b == 0)
    def _():
        start_chunk(0, 0, 0)
        lhs[0:nk, :] = wukt_ref[...]
        newlat[...] = jnp.zeros((page, LANES), F32)
        newkr[...] = jnp.zeros((page, ROPE), F32)
        latb[1] = jnp.zeros((g * page, LANES), BF16)

    qr_bf = qr_ref[0].astype(F32)[:, NOPE:NOPE + ROPE].astype(BF16)
    lhs[nk:nk + nq, :] = qabs_ref[0]
    newlat[0:ts, :] = latn_ref[0]
    newkr[0:ts, :] = krn_ref[0][:, NOPE:NOPE + ROPE]
    m_s[...] = jnp.full((nq, LANES), NEG, F32)
    l_s[...] = jnp.zeros((nq, LANES), F32)
    acc_s[...] = jnp.zeros((nq, LANES), F32)
    lat_bf = newlat[...].astype(BF16)
    s = scores(lat_bf, _dot_nt(qr_bf, newkr[...].astype(BF16)))
    key = lax.broadcasted_iota(jnp.int32, (nq, page), 1)
    qi = lax.broadcasted_iota(jnp.int32, (nq, page), 0) % ts
    s = jnp.where(key <= qi, s, NEG)
    s_scr[1] = jnp.full((nq, g * page), NEG, F32)
    s_scr[1, :, 0:page] = s
    mx_scr[1] = jnp.broadcast_to(jnp.max(s, axis=-1, keepdims=True), (nq, LANES))
    latb[1, 0:page, :] = lat_bf

    def head(slot, nxt_seq, nxt_chunk, nxt_valid):
        parts = []
        for jj in range(g // 2):
            for j in (2 * jj, 2 * jj + 1):
                def issue(j=j):
                    for cp in page_copies(nxt_seq, nxt_chunk, 1 - slot, j):
                        cp.start()
                if nxt_valid is None:
                    issue()
                else:
                    pl.when(nxt_valid)(issue)
            lat_bf = lat_buf[slot, 2 * jj:2 * jj + 2].reshape(2 * page, LANES).astype(BF16)
            latb[slot, 2 * jj * page:(2 * jj + 2) * page, :] = lat_bf
            kr2 = jnp.concatenate([krt_buf[slot, 2 * jj], krt_buf[slot, 2 * jj + 1]], axis=1).astype(BF16)
            parts.append(scores(lat_bf, _dot(qr_bf, kr2)))
        s_scr[slot] = jnp.concatenate(parts, axis=1)
        mx = parts[0]
        for part in parts[1:]:
            mx = jnp.maximum(mx, part)
        mx_scr[slot] = jnp.broadcast_to(jnp.max(mx, axis=-1, keepdims=True), (nq, LANES))

    def pair(cc, carry):
        c0 = cc * 2
        wait_chunk(0)
        tail(1)
        head(0, b, c0 + 1, None)
        wrap = c0 + 2 == nc
        seq2 = jnp.where(wrap, b + 1, b)
        chunk2 = jnp.where(wrap, 0, c0 + 2)
        wait_chunk(1)
        tail(0)
        head(1, seq2, chunk2, seq2 < nb)
        return carry

    for cc in range(nc // 2):
        pair(cc, 0)
    tail(1)
    out_ref[0] = acc_s[...] / jnp.sum(l_s[...], axis=-1, keepdims=True)


def _sample_attn(page_table, qabs, qr, latn, krn, wukt, cache_latent, cache_krope_t, layer, g, ts):
    db, n_pages = page_table.shape
    page = cache_latent.shape[2]
    assert n_pages % (2 * g) == 0
    nq = N_HEADS * ts
    per_seq = lambda rows, w: pl.BlockSpec((1, rows, w), lambda b, pt: (b, 0, 0))
    grid_spec = pltpu.PrefetchScalarGridSpec(
        num_scalar_prefetch=1,
        grid=(db,),
        in_specs=[per_seq(nq, LANES), per_seq(nq, LANES), per_seq(ts, LANES), per_seq(ts, LANES),
                  pl.BlockSpec((N_HEADS * NOPE, LANES), lambda b, pt: (0, 0)),
                  pl.BlockSpec(memory_space=pl.ANY), pl.BlockSpec(memory_space=pl.ANY)],
        out_specs=per_seq(nq, LANES),
        scratch_shapes=[
            pltpu.VMEM((N_HEADS * NOPE + nq, LANES), BF16),
            pltpu.VMEM((page, LANES), F32), pltpu.VMEM((page, ROPE), F32),
            pltpu.VMEM((2, g, page, LANES), F32),
            pltpu.VMEM((2, g, ROPE, page), F32),
            pltpu.VMEM((2, g * page, LANES), BF16),
            pltpu.VMEM((2, nq, g * page), F32),
            pltpu.VMEM((2, nq, LANES), F32),
            pltpu.SemaphoreType.DMA((2, 2)),
            pltpu.VMEM((nq, LANES), F32), pltpu.VMEM((nq, LANES), F32), pltpu.VMEM((nq, LANES), F32),
        ],
    )
    return pl.pallas_call(
        functools.partial(_sample_attn_kernel, layer=layer, g=g, ts=ts, page=page, n_pages=n_pages),
        out_shape=jax.ShapeDtypeStruct((db, nq, LANES), F32),
        grid_spec=grid_spec,
        compiler_params=pltpu.CompilerParams(
            dimension_semantics=("arbitrary",), vmem_limit_bytes=VMEM_LIMIT),
        name="sample_attn",
    )(page_table.reshape(-1), qabs, qr, latn, krn, wukt, cache_latent, cache_krope_t)


def _sample_out_kernel(lo_ref, wuv_ref, gpc_ref, zmg_ref, x_ref, mod_ref, wout_ref, y_ref, *, db, ts, d):
    gate = mod_ref[:, 2 * d:3 * d]
    for t in range(ts):
        ym = jnp.concatenate(
            [_dot(lo_ref[:, (h * ts + t) * LANES:(h * ts + t + 1) * LANES].astype(BF16),
                  wuv_ref[:, h * LANES:(h + 1) * LANES]) for h in range(N_HEADS)], axis=1)
        ymg = (ym * zmg_ref[t * db:(t + 1) * db, :].astype(F32)).astype(BF16)
        o = _dot(gpc_ref[t * db:(t + 1) * db, :], wout_ref[0:512, :]) + _dot(ymg, wout_ref[512:1024, :])
        y_ref[:, t * d:(t + 1) * d] = x_ref[:, t * d:(t + 1) * d] + gate * o


def _sample_out(lat_out2, wuv, gpc, zmg, xs2, mod_s, wout, ts):
    db = xs2.shape[0]
    d = xs2.shape[1] // ts
    return pl.pallas_call(
        functools.partial(_sample_out_kernel, db=db, ts=ts, d=d),
        out_shape=jax.ShapeDtypeStruct(xs2.shape, F32),
        compiler_params=pltpu.CompilerParams(vmem_limit_bytes=VMEM_LIMIT),
        name="sample_out",
    )(lat_out2, wuv, gpc, zmg, xs2, mod_s, wout)


def _rope_tables(pos):
    inv_freq = 1.0 / (ROPE_THETA ** (jnp.arange(0, ROPE, 2, dtype=F32) / ROPE))
    ang = pos.astype(F32)[:, None] * inv_freq[None, :]
    cos, sin = jnp.cos(ang), jnp.sin(ang)
    n = pos.shape[0]
    half = ROPE // 2
    z = lambda w: jnp.zeros((n, w), F32)
    cos_t = jnp.concatenate([jnp.ones((n, NOPE), F32), cos, cos, jnp.ones((n, LANES - NOPE - ROPE), F32)], axis=1)
    sin_a = jnp.concatenate([z(NOPE), -sin, z(LANES - NOPE - half)], axis=1)
    sin_b = jnp.concatenate([z(NOPE + half), sin, z(LANES - NOPE - ROPE)], axis=1)
    return jnp.stack([cos_t, sin_a, sin_b])


def _layer_weights(l, norm_g, w_in, pool_w, pool_scale, conv_w, q_norm_g, w_uq, qn_g, qr_g,
                   kv_norm_g, kr_g, w_uk, kn_g, w_uv, w_out):
    d = w_in.shape[1]
    wi = w_in[l]
    zc = lambda w: jnp.zeros((d, w), F32)
    win = jnp.concatenate([wi[:, 0:1920], zc(NOPE), wi[:, 1920:1952], zc(LANES - NOPE - ROPE), wi[:, 1952:]], axis=1)
    pad_head = lambda w, per: jnp.concatenate(
        [jnp.concatenate([w[:, h * per:(h + 1) * per], jnp.zeros((w.shape[0], LANES - per), F32)], axis=1)
         for h in range(N_HEADS)], axis=1)
    seg = jnp.arange(LANES)
    nope_m = (seg < NOPE).astype(F32)
    rope_m = ((seg >= NOPE) & (seg < NOPE + ROPE)).astype(F32)
    s128k = nope_m[:, None] * nope_m[None, :] / NOPE
    s128q = s128k + rope_m[:, None] * rope_m[None, :] / ROPE
    zl = lambda w: jnp.zeros((w,), F32)
    pw = pool_w[l]
    gw = pw.shape[1]
    poolw = jnp.zeros((256, 256), F32)
    for gi in range(pw.shape[0]):
        poolw = poolw.at[gi * gw:(gi + 1) * gw, gi * gw:(gi + 1) * gw].set(pw[gi])
    wukt_pad = jnp.stack([jnp.concatenate([w_uk[l][:, h * NOPE:(h + 1) * NOPE].T,
                                           jnp.zeros((LANES - NOPE, w_uk.shape[1]), F32)], axis=0)
                          for h in range(N_HEADS)])
    return {
        "norm_g": norm_g[l][None, :],
        "win": win.astype(BF16),
        "poolw": poolw.astype(BF16),
        "pscale": pool_scale[l][None, :],
        "convw": conv_w[l],
        "qng": q_norm_g[l][None, :],
        "wuq": pad_head(w_uq[l], NOPE + ROPE).astype(BF16),
        "s128q": s128q.astype(BF16),
        "s128k": s128k.astype(BF16),
        "gq": (jnp.concatenate([qn_g[l], qr_g[l], zl(LANES - NOPE - ROPE)]) * SM_SCALE)[None, :],
        "kvg": kv_norm_g[l][None, :],
        "krg": jnp.concatenate([zl(NOPE), kr_g[l], zl(LANES - NOPE - ROPE)])[None, :],
        "wuk": pad_head(w_uk[l], NOPE).astype(BF16),
        "gk": jnp.concatenate([kn_g[l], zl(LANES - NOPE)])[None, :],
        "kngpad": jnp.concatenate([kn_g[l], zl(LANES - NOPE)])[None, :],
        "wuv": w_uv[l].astype(BF16),
        "wukt": w_uk[l].T.astype(BF16),
        "wukt_pad": wukt_pad.astype(BF16),
        "wout": w_out[l].astype(BF16),
    }


def kernel(x_prompt, x_sample, cache_latent, cache_krope, state_pool, state_conv, page_table,
           c_prompt, c_sample, norm_g, w_ada, b_ada, w_in, pool_w, pool_scale, conv_w,
           q_norm_g, w_uq, qn_g, qr_g, kv_norm_g, kr_g, w_uk, kn_g, w_uv, w_out):
    depth = w_in.shape[0]
    bp, seq, d = x_prompt.shape
    db, ts, _ = x_sample.shape
    page = cache_latent.shape[2]
    past_len = page_table.shape[1] * page
    ts_p = 512 if seq % 512 == 0 else seq
    tq = 512 if seq % 512 == 0 else seq
    g_pages = 32
    cache_krope_t = jnp.swapaxes(cache_krope, 2, 3)

    mod = _modulation(jnp.concatenate([c_prompt, c_sample], axis=0), w_ada, b_ada)
    tab_p = _rope_tables(jnp.arange(seq, dtype=jnp.int32))
    tab_s = jnp.repeat(_rope_tables(past_len + jnp.arange(ts, dtype=jnp.int32)), db, axis=1)
    lws = [_layer_weights(l, norm_g, w_in, pool_w, pool_scale, conv_w, q_norm_g, w_uq, qn_g, qr_g,
                          kv_norm_g, kr_g, w_uk, kn_g, w_uv, w_out) for l in range(depth)]

    xp = x_prompt
    lat_p, kr_p, pool_p, conv_p = [], [], [], []
    for l in range(depth):
        mod_p = mod[l, :bp].reshape(bp, 3, d)
        gpc, zmg, qp, kp, v, lat, kr, pst, cst = _prompt_in(xp, mod_p, lws[l], tab_p, ts_p)
        xp = _prompt_out(qp, kp, v, gpc, zmg, xp, mod_p, lws[l]["wout"], tq)
        lat_p.append(lat)
        kr_p.append(kr)
        pool_p.append(pst)
        conv_p.append(cst)

    xs2 = x_sample.reshape(db, ts * d)
    lat_s, kr_s, pool_s, conv_s = [], [], [], []
    for l in range(depth):
        mod_s = mod[l, bp:]
        pst = state_pool[l].reshape(db, -1)
        cst = state_conv[l].reshape(db, -1)
        gpc, zmg, qabs, qr, latn, krn, psto, csto = _sample_in(xs2, mod_s, lws[l], tab_s, pst, cst, ts, past_len)
        nq = N_HEADS * ts
        lat_out = _sample_attn(page_table, qabs.reshape(db, nq, LANES), qr.reshape(db, nq, LANES),
                               latn.reshape(db, ts, LANES), krn.reshape(db, ts, LANES), lws[l]["wukt"],
                               cache_latent, cache_krope_t, l, g_pages, ts)
        xs2 = _sample_out(lat_out.reshape(db, nq * LANES), lws[l]["wuv"], gpc, zmg, xs2, mod_s,
                          lws[l]["wout"], ts)
        lat_s.append(latn.reshape(db, ts, LANES))
        kr_s.append(krn.reshape(db, ts, LANES)[:, :, NOPE:NOPE + ROPE])
        pool_s.append(psto.reshape(state_pool.shape[1:]))
        conv_s.append(csto.reshape(state_conv.shape[1:]))

    return (xp, xs2.reshape(x_sample.shape),
            jnp.stack(lat_p), jnp.stack(kr_p), jnp.stack(pool_p), jnp.stack(conv_p),
            jnp.stack(lat_s), jnp.stack(kr_s), jnp.stack(pool_s), jnp.stack(conv_s))
```

```python
import functools
import math

import jax
import jax.numpy as jnp
from jax import lax
from jax.experimental import pallas as pl
from jax.experimental.pallas import tpu as pltpu

F32 = jnp.float32
BF16 = jnp.bfloat16

POOL_WINDOWS = (2, 4, 8, 16)
POOL_MAX = max(POOL_WINDOWS)
CONV_W = 3
N_HEADS = 4
NOPE = 64
ROPE = 32
ROPE_THETA = 10000.0
EPS = 1e-6
NEG = -1e30
SM_SCALE = 1.0 / math.sqrt(NOPE + ROPE)
LANES = 128
HALO = 16
VMEM_LIMIT = 56 * 1024 * 1024

C_POOL = 0
C_CONV = 512
C_MLA = 1536
C_ZM = 2048
D_IN_PAD = 2560


def _sigmoid(x):
    return 1.0 / (1.0 + jnp.exp(-x))


def _silu(x):
    return x * _sigmoid(x)


def _dot(a, b):
    return jnp.dot(a, b, preferred_element_type=F32)


def _dot_nt(a, b):
    return lax.dot_general(a, b, (((1,), (1,)), ((), ())), preferred_element_type=F32)


def _rms(x, n):
    return x * lax.rsqrt(jnp.sum(x * x, axis=-1, keepdims=True) * (1.0 / n) + EPS)


def _rope_block(x, cos_t, sin_a, sin_b):
    return x * cos_t + pltpu.roll(x, LANES - ROPE // 2, 1) * sin_a + pltpu.roll(x, ROPE // 2, 1) * sin_b


def _mod_kernel(c_ref, w_ref, b_ref, o_ref):
    a = _silu(c_ref[...]).astype(BF16)
    o_ref[0] = _dot(a, w_ref[0].astype(BF16)) + b_ref[0]


def _modulation(c_all, w_ada, b_ada):
    depth, d, d3 = w_ada.shape
    r = c_all.shape[0]
    nb = d3 // d
    return pl.pallas_call(
        _mod_kernel,
        out_shape=jax.ShapeDtypeStruct((depth, r, d3), F32),
        grid=(depth, nb),
        in_specs=[
            pl.BlockSpec((r, d), lambda l, j: (0, 0)),
            pl.BlockSpec((1, d, d), lambda l, j: (l, 0, j)),
            pl.BlockSpec((1, 1, d), lambda l, j: (l, 0, j)),
        ],
        out_specs=pl.BlockSpec((1, r, d), lambda l, j: (l, 0, j)),
        compiler_params=pltpu.CompilerParams(vmem_limit_bytes=VMEM_LIMIT),
        name="adaln_mod",
    )(c_all, w_ada, b_ada.reshape(depth, 1, d3))


def _mla_prep(pm, qng, wuq, s128q, gq, kvg, krg, cos_t, sin_a, sin_b):
    cq = pm[:, 0:256]
    ckv_raw = pm[:, 256:384]
    kr_raw = pm[:, 384:512]
    cqn = (_rms(cq, 256) * qng).astype(BF16)
    q = _dot(cqn, wuq)
    qh = []
    for h in range(N_HEADS):
        xh = q[:, h * LANES:(h + 1) * LANES]
        ms = _dot((xh * xh).astype(BF16), s128q)
        xh = xh * lax.rsqrt(ms + EPS) * gq
        qh.append(_rope_block(xh, cos_t, sin_a, sin_b))
    ckv = _rms(ckv_raw, 128) * kvg
    krn = _rms(kr_raw, ROPE) * krg
    kr = _rope_block(krn, cos_t, sin_a, sin_b)
    return qh, ckv, kr


def _prompt_in_kernel(x_ref, mod_ref, g_ref, win_ref, poolw_ref, pscale_ref, convw_ref,
                      qng_ref, wuq_ref, s128q_ref, gq_ref, kvg_ref, krg_ref,
                      wuk_ref, s128k_ref, gk_ref, wuv_ref, tab_ref,
                      gpc_ref, zmg_ref, qp_ref, kp_ref, v_ref, lat_ref, kr_ref, pst_ref, cst_ref,
                      uext, vext, *, ts, nt):
    t = pl.program_id(1)
    x = x_ref[0]
    ms = jnp.mean(x * x, axis=-1, keepdims=True)
    h = (x * lax.rsqrt(ms + EPS) * g_ref[...]) * (1.0 + mod_ref[0, 1:2, :]) + mod_ref[0, 0:1, :]
    hb = h.astype(BF16)

    @pl.when(t == 0)
    def _():
        uext[0:HALO, :] = jnp.zeros((HALO, 256), F32)
        vext[0:HALO, :] = jnp.zeros((HALO, 256), F32)

    pp = _dot(hb, win_ref[:, C_POOL:C_POOL + 512])
    u = pp[:, 0:256]
    zp = pp[:, 256:512]
    uext[HALO:HALO + ts, :] = u
    ext = uext[...]
    a1 = ext + pltpu.roll(ext, 1, 0)
    a2 = a1 + pltpu.roll(a1, 2, 0)
    a3 = a2[:, 128:256]
    a3 = a3 + pltpu.roll(a3, 4, 0)
    a4 = a3 + pltpu.roll(a3, 8, 0)
    pos1 = t * ts + lax.broadcasted_iota(jnp.int32, (ts, 1), 0) + 1
    inv = [1.0 / jnp.minimum(pos1, w).astype(F32) for w in POOL_WINDOWS]
    lane = lax.broadcasted_iota(jnp.int32, (1, LANES), 1)
    mean_l = jnp.where(lane < 64, a1[HALO:, 0:128] * inv[0], a2[HALO:, 0:128] * inv[1])
    mean_r = jnp.where(lane < 64, a3[HALO:] * inv[2], a4[HALO:] * inv[3])
    d = jnp.concatenate([mean_l, mean_r], axis=1) - u
    yp = _dot(d.astype(BF16), poolw_ref[...]) * pscale_ref[...]
    gp = yp * _silu(zp)

    @pl.when(t == nt - 1)
    def _():
        pst_ref[0] = uext[pl.ds(HALO + ts - (POOL_MAX - 1), POOL_MAX - 1), :]

    uext[0:HALO, :] = u[ts - HALO:ts, :]

    pc = _dot(hb, win_ref[:, C_CONV:C_CONV + 1024])
    hc = pc[:, 0:256]
    bg = pc[:, 256:512]
    cg = pc[:, 512:768]
    zc = pc[:, 768:1024]
    v = cg * hc
    vext[HALO:HALO + ts, :] = v
    e = vext[...]
    y = convw_ref[0:1, :] * pltpu.roll(e, 2, 0) + convw_ref[1:2, :] * pltpu.roll(e, 1, 0) + convw_ref[2:3, :] * e
    gc = bg * y[HALO:] * _silu(zc)
    gpc_ref[0] = jnp.concatenate([gp, gc], axis=1).astype(BF16)

    @pl.when(t == nt - 1)
    def _():
        cst_ref[0] = vext[pl.ds(HALO + ts - (CONV_W - 1), CONV_W - 1), :]

    vext[0:HALO, :] = v[ts - HALO:ts, :]

    zmg_ref[0] = _silu(_dot(hb, win_ref[:, C_ZM:C_ZM + 512])).astype(BF16)

    pm = _dot(hb, win_ref[:, C_MLA:C_MLA + 512])
    cos_t = tab_ref[0]
    sin_a = tab_ref[1]
    sin_b = tab_ref[2]
    qh, ckv, kr = _mla_prep(pm, qng_ref[...], wuq_ref[...], s128q_ref[...], gq_ref[...],
                            kvg_ref[...], krg_ref[...], cos_t, sin_a, sin_b)
    qp_ref[0] = jnp.concatenate(qh, axis=1).astype(BF16)
    lat_ref[0] = ckv
    kr_ref[0] = kr[:, NOPE:NOPE + ROPE]
    cb = ckv.astype(BF16)
    kc = _dot(cb, wuk_ref[...])
    kh = []
    for hh in range(N_HEADS):
        xh = kc[:, hh * LANES:(hh + 1) * LANES]
        msk = _dot((xh * xh).astype(BF16), s128k_ref[...])
        kh.append(xh * lax.rsqrt(msk + EPS) * gk_ref[...] + kr)
    kp_ref[0] = jnp.concatenate(kh, axis=1).astype(BF16)
    v_ref[0] = _dot(cb, wuv_ref[...]).astype(BF16)


def _prompt_in(x, mod_p, lw, tab, ts):
    b, t, d = x.shape
    nt = t // ts
    const = lambda shape: pl.BlockSpec(shape, lambda i, j: (0,) * len(shape))
    tile = lambda w: pl.BlockSpec((1, ts, w), lambda i, j: (i, j, 0))
    outs = [
        jax.ShapeDtypeStruct((b, t, 512), BF16),
        jax.ShapeDtypeStruct((b, t, 512), BF16),
        jax.ShapeDtypeStruct((b, t, 512), BF16),
        jax.ShapeDtypeStruct((b, t, 512), BF16),
        jax.ShapeDtypeStruct((b, t, 512), BF16),
        jax.ShapeDtypeStruct((b, t, 128), F32),
        jax.ShapeDtypeStruct((b, t, ROPE), F32),
        jax.ShapeDtypeStruct((b, POOL_MAX - 1, 256), F32),
        jax.ShapeDtypeStruct((b, CONV_W - 1, 256), F32),
    ]
    return pl.pallas_call(
        functools.partial(_prompt_in_kernel, ts=ts, nt=nt),
        out_shape=outs,
        grid=(b, nt),
        in_specs=[
            tile(d),
            pl.BlockSpec((1, 3, d), lambda i, j: (i, 0, 0)),
            const((1, d)),
            const((d, D_IN_PAD)),
            const((256, 256)), const((1, 256)), const((CONV_W, 256)),
            const((1, 256)), const((256, 512)), const((128, 128)), const((1, 128)),
            const((1, 128)), const((1, 128)),
            const((128, 512)), const((128, 128)), const((1, 128)), const((128, 512)),
            pl.BlockSpec((3, ts, 128), lambda i, j: (0, j, 0)),
        ],
        out_specs=[
            tile(512), tile(512), tile(512), tile(512), tile(512), tile(128), tile(ROPE),
            pl.BlockSpec((1, POOL_MAX - 1, 256), lambda i, j: (i, 0, 0)),
            pl.BlockSpec((1, CONV_W - 1, 256), lambda i, j: (i, 0, 0)),
        ],
        scratch_shapes=[pltpu.VMEM((HALO + ts, 256), F32), pltpu.VMEM((HALO + ts, 256), F32)],
        compiler_params=pltpu.CompilerParams(
            dimension_semantics=("parallel", "arbitrary"), vmem_limit_bytes=VMEM_LIMIT),
        name="prompt_in",
    )(x, mod_p, lw["norm_g"], lw["win"], lw["poolw"], lw["pscale"], lw["convw"],
      lw["qng"], lw["wuq"], lw["s128q"], lw["gq"], lw["kvg"], lw["krg"],
      lw["wuk"], lw["s128k"], lw["gk"], lw["wuv"], tab)


def _prompt_out_kernel(q_ref, k_ref, v_ref, gpc_ref, zmg_ref, x_ref, gate_ref, wout_ref, y_ref,
                       s_scr, p_scr, ym_scr, *, tq, nt):
    i = pl.program_id(1)
    row = lax.broadcasted_iota(jnp.int32, (tq, LANES), 0)
    lane = lax.broadcasted_iota(jnp.int32, (tq, LANES), 1)
    n_diag = tq // LANES

    def attend(n_keys):
        n_pieces = n_keys // LANES
        for h in range(N_HEADS):
            par = h % 2
            sl = slice(h * LANES, (h + 1) * LANES)
            s_scr[par, :, 0:n_keys] = _dot_nt(q_ref[0, :, sl], k_ref[0, 0:n_keys, sl])

            def piece(t):
                x = s_scr[par, :, t * LANES:(t + 1) * LANES]
                td = t - (n_pieces - n_diag)
                if td >= 0:
                    x = jnp.where(td * LANES + lane <= row, x, NEG)
                return x

            mx = piece(0)
            for t in range(1, n_pieces):
                mx = jnp.maximum(mx, piece(t))
            m_rep = jnp.broadcast_to(jnp.max(mx, axis=-1, keepdims=True), (tq, LANES))
            psum = jnp.zeros((tq, LANES), F32)
            for t in range(n_pieces):
                p_t = jnp.exp(piece(t) - m_rep)
                psum = psum + p_t
                p_scr[par, :, t * LANES:(t + 1) * LANES] = p_t.astype(BF16)
            o = _dot(p_scr[par, :, 0:n_keys], v_ref[0, 0:n_keys, sl])
            ym_scr[:, sl] = o / jnp.sum(psum, axis=-1, keepdims=True)

    for v in range(nt):
        pl.when(i == v)(functools.partial(attend, (v + 1) * tq))

    ym = ym_scr[...] * zmg_ref[0].astype(F32)
    o = _dot(gpc_ref[0], wout_ref[0:512, :]) + _dot(ym.astype(BF16), wout_ref[512:1024, :])
    y_ref[0] = x_ref[0] + gate_ref[0, 2:3, :] * o


def _prompt_out(qp, kp, v, gpc, zmg, x, mod_p, wout, tq):
    b, t, d = x.shape
    tile = lambda w: pl.BlockSpec((1, tq, w), lambda i, j: (i, j, 0))
    full = lambda w: pl.BlockSpec((1, t, w), lambda i, j: (i, 0, 0))
    return pl.pallas_call(
        functools.partial(_prompt_out_kernel, tq=tq, nt=t // tq),
        out_shape=jax.ShapeDtypeStruct((b, t, d), F32),
        grid=(b, t // tq),
        in_specs=[
            tile(512), full(512), full(512), tile(512), tile(512), tile(d),
            pl.BlockSpec((1, 3, d), lambda i, j: (i, 0, 0)),
            pl.BlockSpec((d, d), lambda i, j: (0, 0)),
        ],
        out_specs=tile(d),
        scratch_shapes=[pltpu.VMEM((2, tq, t), F32), pltpu.VMEM((2, tq, t), BF16),
                        pltpu.VMEM((tq, N_HEADS * LANES), F32)],
        compiler_params=pltpu.CompilerParams(
            dimension_semantics=("parallel", "arbitrary"), vmem_limit_bytes=VMEM_LIMIT),
        name="prompt_out",
    )(qp, kp, v, gpc, zmg, x, mod_p, wout)


def _sample_in_kernel(x_ref, mod_ref, g_ref, win_ref, poolw_ref, pscale_ref, convw_ref,
                      qng_ref, wuq_ref, s128q_ref, gq_ref, kvg_ref, krg_ref, kng_ref, wukt_ref, tab_ref,
                      pst_ref, cst_ref,
                      gpc_ref, zmg_ref, qabs_ref, qr_ref, lat_ref, krp_ref, psto_ref, csto_ref,
                      *, db, ts, d, past_len):
    shift = mod_ref[:, 0:d]
    scale1 = 1.0 + mod_ref[:, d:2 * d]
    hs = []
    for t in range(ts):
        x = x_ref[:, t * d:(t + 1) * d]
        ms = jnp.mean(x * x, axis=-1, keepdims=True)
        hs.append(((x * lax.rsqrt(ms + EPS) * g_ref[...]) * scale1 + shift).astype(BF16))
    hb = jnp.concatenate(hs, axis=0)
    rows = lambda a, t: a[t * db:(t + 1) * db]

    pp = _dot(hb, win_ref[:, C_POOL:C_POOL + 512])
    zp = pp[:, 256:512]
    np_ = POOL_MAX - 1
    ext = [pst_ref[:, k * 256:(k + 1) * 256] for k in range(np_)] + [rows(pp, t)[:, 0:256] for t in range(ts)]
    for k in range(np_):
        psto_ref[:, k * 256:(k + 1) * 256] = ext[ts + k]

    def doubled(prev, step, lo):
        return {r: prev[r] + prev[r - step] for r in range(lo, np_ + ts) if r in prev and (r - step) in prev}

    a0 = dict(enumerate(ext))
    a1 = doubled(a0, 1, 1)
    a2 = doubled(a1, 2, 3)
    a3 = doubled(a2, 4, 7)
    a4 = doubled(a3, 8, 15)
    lane = lax.broadcasted_iota(jnp.int32, (1, 256), 1)
    ds_ = []
    for t in range(ts):
        r = np_ + t
        inv = [1.0 / float(min(past_len + t + 1, w)) for w in POOL_WINDOWS]
        mean = jnp.where(lane < 64, a1[r] * inv[0],
                         jnp.where(lane < 128, a2[r] * inv[1],
                                   jnp.where(lane < 192, a3[r] * inv[2], a4[r] * inv[3])))
        ds_.append(mean - ext[r])
    dcat = jnp.concatenate(ds_, axis=0)
    gp = _dot(dcat.astype(BF16), poolw_ref[...]) * pscale_ref[...] * _silu(zp)

    pc = _dot(hb, win_ref[:, C_CONV:C_CONV + 1024])
    bg = pc[:, 256:512]
    zc = pc[:, 768:1024]
    vfull = pc[:, 512:768] * pc[:, 0:256]
    nc = CONV_W - 1
    vx = [cst_ref[:, k * 256:(k + 1) * 256] for k in range(nc)] + [rows(vfull, t) for t in range(ts)]
    for k in range(nc):
        csto_ref[:, k * 256:(k + 1) * 256] = vx[ts + k]
    ys = []
    for t in range(ts):
        acc = convw_ref[0:1, :] * vx[t]
        for k in range(1, CONV_W):
            acc = acc + convw_ref[k:k + 1, :] * vx[t + k]
        ys.append(acc)
    gc = bg * jnp.concatenate(ys, axis=0) * _silu(zc)
    gpc_ref[...] = jnp.concatenate([gp, gc], axis=1).astype(BF16)

    zmg_ref[...] = _silu(_dot(hb, win_ref[:, C_ZM:C_ZM + 512])).astype(BF16)

    pm = _dot(hb, win_ref[:, C_MLA:C_MLA + 512])
    qh, ckv, kr = _mla_prep(pm, qng_ref[...], wuq_ref[...], s128q_ref[...], gq_ref[...],
                            kvg_ref[...], krg_ref[...], tab_ref[0], tab_ref[1], tab_ref[2])
    for t in range(ts):
        lat_ref[:, t * LANES:(t + 1) * LANES] = rows(ckv, t)
        krp_ref[:, t * LANES:(t + 1) * LANES] = rows(kr, t)
    for h in range(N_HEADS):
        qa = _dot((qh[h] * kng_ref[...]).astype(BF16), wukt_ref[h])
        for t in range(ts):
            c0 = (h * ts + t) * LANES
            qabs_ref[:, c0:c0 + LANES] = rows(qa, t).astype(BF16)
            qr_ref[:, c0:c0 + LANES] = rows(qh[h], t).astype(BF16)


def _sample_in(xs2, mod_s, lw, tab, pst, cst, ts, past_len):
    db = xs2.shape[0]
    d = xs2.shape[1] // ts
    r = ts * db
    outs = [
        jax.ShapeDtypeStruct((r, 512), BF16),
        jax.ShapeDtypeStruct((r, 512), BF16),
        jax.ShapeDtypeStruct((db, N_HEADS * ts * LANES), BF16),
        jax.ShapeDtypeStruct((db, N_HEADS * ts * LANES), BF16),
        jax.ShapeDtypeStruct((db, ts * LANES), F32),
        jax.ShapeDtypeStruct((db, ts * LANES), F32),
        jax.ShapeDtypeStruct(pst.shape, F32),
        jax.ShapeDtypeStruct(cst.shape, F32),
    ]
    return pl.pallas_call(
        functools.partial(_sample_in_kernel, db=db, ts=ts, d=d, past_len=past_len),
        out_shape=outs,
        compiler_params=pltpu.CompilerParams(vmem_limit_bytes=VMEM_LIMIT),
        name="sample_in",
    )(xs2, mod_s, lw["norm_g"], lw["win"], lw["poolw"], lw["pscale"], lw["convw"],
      lw["qng"], lw["wuq"], lw["s128q"], lw["gq"], lw["kvg"], lw["krg"], lw["kngpad"], lw["wukt_pad"], tab,
      pst, cst)


def _sample_attn_kernel(pt_ref, qabs_ref, qr_ref, latn_ref, krn_ref, wukt_ref, lat_hbm, krt_hbm, out_ref,
                        lhs, newlat, newkr, lat_buf, krt_buf, latb, s_scr, mx_scr, sem, m_s, l_s, acc_s,
                        *, layer, g, ts, page, n_pages):
    b = pl.program_id(0)
    nb = pl.num_programs(0)
    nc = n_pages // g
    nq = N_HEADS * ts
    nk = N_HEADS * NOPE
    AHEAD = 2

    def page_copies(seq, chunk, slot, j):
        pg = pt_ref[seq * n_pages + chunk * g + j]
        return (pltpu.make_async_copy(lat_hbm.at[layer, pg], lat_buf.at[slot, j], sem.at[0, slot]),
                pltpu.make_async_copy(krt_hbm.at[layer, pg], krt_buf.at[slot, j], sem.at[1, slot]))

    def start_chunk(seq, chunk, slot):
        for j in range(g):
            for cp in page_copies(seq, chunk, slot, j):
                cp.start()

    def wait_chunk(slot):
        pltpu.make_async_copy(lat_hbm.at[layer, pl.ds(0, g)], lat_buf.at[slot], sem.at[0, slot]).wait()
        pltpu.make_async_copy(krt_hbm.at[layer, pl.ds(0, g)], krt_buf.at[slot], sem.at[1, slot]).wait()

    def scores(lat_bf, s_rope):
        res = _dot_nt(lhs[...], lat_bf)
        kt = res[0:nk]
        k2 = kt * kt
        rr = []
        for h in range(N_HEADS):
            ss = jnp.sum(k2[h * NOPE:(h + 1) * NOPE], axis=0, keepdims=True)
            rr.append(jnp.broadcast_to(lax.rsqrt(ss * (1.0 / NOPE) + EPS), (ts, lat_bf.shape[0])))
        return res[nk:nk + nq] * jnp.concatenate(rr, axis=0) + s_rope

    def tail(slot):
        m_old = m_s[...]
        m_new = jnp.maximum(m_old, mx_scr[slot])
        corr = jnp.exp(m_old - m_new)
        psum = jnp.zeros((nq, LANES), F32)
        ps = []
        for t in range(g * page // LANES):
            p_t = jnp.exp(s_scr[slot, :, t * LANES:(t + 1) * LANES] - m_new)
            psum = psum + p_t
            ps.append(p_t.astype(BF16))
        l_s[...] = l_s[...] * corr + psum
        acc_s[...] = acc_s[...] * corr + _dot(jnp.concatenate(ps, axis=1), latb[slot])
        m_s[...] = m_new

    @pl.when(b == 0)
    def _():
        for c in range(AHEAD):
            start_chunk(0, c, c)
        lhs[0:nk, :] = wukt_ref[...]
        newlat[...] = jnp.zeros((page, LANES), F32)
        newkr[...] = jnp.zeros((page, ROPE), F32)
        latb[1] = jnp.zeros((g * page, LANES), BF16)

    qr_bf = qr_ref[0].astype(F32)[:, NOPE:NOPE + ROPE].astype(BF16)
    lhs[nk:nk + nq, :] = qabs_ref[0]
    newlat[0:ts, :] = latn_ref[0]
    newkr[0:ts, :] = krn_ref[0][:, NOPE:NOPE + ROPE]
    m_s[...] = jnp.full((nq, LANES), NEG, F32)
    l_s[...] = jnp.zeros((nq, LANES), F32)
    acc_s[...] = jnp.zeros((nq, LANES), F32)
    lat_bf = newlat[...].astype(BF16)
    s = scores(lat_bf, _dot_nt(qr_bf, newkr[...].astype(BF16)))
    key = lax.broadcasted_iota(jnp.int32, (nq, page), 1)
    qi = lax.broadcasted_iota(jnp.int32, (nq, page), 0) % ts
    s = jnp.where(key <= qi, s, NEG)
    s_scr[1] = jnp.full((nq, g * page), NEG, F32)
    s_scr[1, :, 0:page] = s
    mx_scr[1] = jnp.broadcast_to(jnp.max(s, axis=-1, keepdims=True), (nq, LANES))
    latb[1, 0:page, :] = lat_bf

    def head(c, nxt_seq, nxt_chunk, nxt_valid):
        par = c % 2
        parts = []
        for jj in range(g // 2):
            for j in (2 * jj, 2 * jj + 1):
                def issue(j=j):
                    for cp in page_copies(nxt_seq, nxt_chunk, nxt_chunk, j):
                        cp.start()
                if nxt_valid is None:
                    issue()
                else:
                    pl.when(nxt_valid)(issue)
            lat_bf = lat_buf[c, 2 * jj:2 * jj + 2].reshape(2 * page, LANES).astype(BF16)
            latb[par, 2 * jj * page:(2 * jj + 2) * page, :] = lat_bf
            kr2 = jnp.concatenate([krt_buf[c, 2 * jj], krt_buf[c, 2 * jj + 1]], axis=1).astype(BF16)
            parts.append(scores(lat_bf, _dot(qr_bf, kr2)))
        s_scr[par] = jnp.concatenate(parts, axis=1)
        mx = parts[0]
        for part in parts[1:]:
            mx = jnp.maximum(mx, part)
        mx_scr[par] = jnp.broadcast_to(jnp.max(mx, axis=-1, keepdims=True), (nq, LANES))

    for c in range(nc):
        wait_chunk(c)
        tail(1 - c % 2)
        if c + AHEAD < nc:
            head(c, b, c + AHEAD, None)
        else:
            head(c, b + 1, c + AHEAD - nc, b + 1 < nb)
    tail((nc - 1) % 2)
    out_ref[0] = acc_s[...] / jnp.sum(l_s[...], axis=-1, keepdims=True)


def _sample_attn(page_table, qabs, qr, latn, krn, wukt, cache_latent, cache_krope_t, layer, g, ts):
    db, n_pages = page_table.shape
    page = cache_latent.shape[2]
    nc = n_pages // g
    assert n_pages % g == 0 and nc % 2 == 0 and nc > 2
    nq = N_HEADS * ts
    per_seq = lambda rows, w: pl.BlockSpec((1, rows, w), lambda b, pt: (b, 0, 0))
    grid_spec = pltpu.PrefetchScalarGridSpec(
        num_scalar_prefetch=1,
        grid=(db,),
        in_specs=[per_seq(nq, LANES), per_seq(nq, LANES), per_seq(ts, LANES), per_seq(ts, LANES),
                  pl.BlockSpec((N_HEADS * NOPE, LANES), lambda b, pt: (0, 0)),
                  pl.BlockSpec(memory_space=pl.ANY), pl.BlockSpec(memory_space=pl.ANY)],
        out_specs=per_seq(nq, LANES),
        scratch_shapes=[
            pltpu.VMEM((N_HEADS * NOPE + nq, LANES), BF16),
            pltpu.VMEM((page, LANES), F32), pltpu.VMEM((page, ROPE), F32),
            pltpu.VMEM((nc, g, page, LANES), F32),
            pltpu.VMEM((nc, g, ROPE, page), F32),
            pltpu.VMEM((2, g * page, LANES), BF16),
            pltpu.VMEM((2, nq, g * page), F32),
            pltpu.VMEM((2, nq, LANES), F32),
            pltpu.SemaphoreType.DMA((2, nc)),
            pltpu.VMEM((nq, LANES), F32), pltpu.VMEM((nq, LANES), F32), pltpu.VMEM((nq, LANES), F32),
        ],
    )
    return pl.pallas_call(
        functools.partial(_sample_attn_kernel, layer=layer, g=g, ts=ts, page=page, n_pages=n_pages),
        out_shape=jax.ShapeDtypeStruct((db, nq, LANES), F32),
        grid_spec=grid_spec,
        compiler_params=pltpu.CompilerParams(
            dimension_semantics=("arbitrary",), vmem_limit_bytes=VMEM_LIMIT),
        name="sample_attn",
    )(page_table.reshape(-1), qabs, qr, latn, krn, wukt, cache_latent, cache_krope_t)


def _sample_out_kernel(lo_ref, wuv_ref, gpc_ref, zmg_ref, x_ref, mod_ref, wout_ref, y_ref, *, db, ts, d):
    gate = mod_ref[:, 2 * d:3 * d]
    for t in range(ts):
        ym = jnp.concatenate(
            [_dot(lo_ref[:, (h * ts + t) * LANES:(h * ts + t + 1) * LANES].astype(BF16),
                  wuv_ref[:, h * LANES:(h + 1) * LANES]) for h in range(N_HEADS)], axis=1)
        ymg = (ym * zmg_ref[t * db:(t + 1) * db, :].astype(F32)).astype(BF16)
        o = _dot(gpc_ref[t * db:(t + 1) * db, :], wout_ref[0:512, :]) + _dot(ymg, wout_ref[512:1024, :])
        y_ref[:, t * d:(t + 1) * d] = x_ref[:, t * d:(t + 1) * d] + gate * o


def _sample_out(lat_out2, wuv, gpc, zmg, xs2, mod_s, wout, ts):
    db = xs2.shape[0]
    d = xs2.shape[1] // ts
    return pl.pallas_call(
        functools.partial(_sample_out_kernel, db=db, ts=ts, d=d),
        out_shape=jax.ShapeDtypeStruct(xs2.shape, F32),
        compiler_params=pltpu.CompilerParams(vmem_limit_bytes=VMEM_LIMIT),
        name="sample_out",
    )(lat_out2, wuv, gpc, zmg, xs2, mod_s, wout)


def _rope_tables(pos):
    inv_freq = 1.0 / (ROPE_THETA ** (jnp.arange(0, ROPE, 2, dtype=F32) / ROPE))
    ang = pos.astype(F32)[:, None] * inv_freq[None, :]
    cos, sin = jnp.cos(ang), jnp.sin(ang)
    n = pos.shape[0]
    half = ROPE // 2
    z = lambda w: jnp.zeros((n, w), F32)
    cos_t = jnp.concatenate([jnp.ones((n, NOPE), F32), cos, cos, jnp.ones((n, LANES - NOPE - ROPE), F32)], axis=1)
    sin_a = jnp.concatenate([z(NOPE), -sin, z(LANES - NOPE - half)], axis=1)
    sin_b = jnp.concatenate([z(NOPE + half), sin, z(LANES - NOPE - ROPE)], axis=1)
    return jnp.stack([cos_t, sin_a, sin_b])


def _layer_weights(l, norm_g, w_in, pool_w, pool_scale, conv_w, q_norm_g, w_uq, qn_g, qr_g,
                   kv_norm_g, kr_g, w_uk, kn_g, w_uv, w_out):
    d = w_in.shape[1]
    wi = w_in[l]
    zc = lambda w: jnp.zeros((d, w), F32)
    win = jnp.concatenate([wi[:, 0:1920], zc(NOPE), wi[:, 1920:1952], zc(LANES - NOPE - ROPE), wi[:, 1952:]], axis=1)
    pad_head = lambda w, per: jnp.concatenate(
        [jnp.concatenate([w[:, h * per:(h + 1) * per], jnp.zeros((w.shape[0], LANES - per), F32)], axis=1)
         for h in range(N_HEADS)], axis=1)
    seg = jnp.arange(LANES)
    nope_m = (seg < NOPE).astype(F32)
    rope_m = ((seg >= NOPE) & (seg < NOPE + ROPE)).astype(F32)
    s128k = nope_m[:, None] * nope_m[None, :] / NOPE
    s128q = s128k + rope_m[:, None] * rope_m[None, :] / ROPE
    zl = lambda w: jnp.zeros((w,), F32)
    pw = pool_w[l]
    gw = pw.shape[1]
    poolw = jnp.zeros((256, 256), F32)
    for gi in range(pw.shape[0]):
        poolw = poolw.at[gi * gw:(gi + 1) * gw, gi * gw:(gi + 1) * gw].set(pw[gi])
    wukt_pad = jnp.stack([jnp.concatenate([w_uk[l][:, h * NOPE:(h + 1) * NOPE].T,
                                           jnp.zeros((LANES - NOPE, w_uk.shape[1]), F32)], axis=0)
                          for h in range(N_HEADS)])
    return {
        "norm_g": norm_g[l][None, :],
        "win": win.astype(BF16),
        "poolw": poolw.astype(BF16),
        "pscale": pool_scale[l][None, :],
        "convw": conv_w[l],
        "qng": q_norm_g[l][None, :],
        "wuq": pad_head(w_uq[l], NOPE + ROPE).astype(BF16),
        "s128q": s128q.astype(BF16),
        "s128k": s128k.astype(BF16),
        "gq": (jnp.concatenate([qn_g[l], qr_g[l], zl(LANES - NOPE - ROPE)]) * SM_SCALE)[None, :],
        "kvg": kv_norm_g[l][None, :],
        "krg": jnp.concatenate([zl(NOPE), kr_g[l], zl(LANES - NOPE - ROPE)])[None, :],
        "wuk": pad_head(w_uk[l], NOPE).astype(BF16),
        "gk": jnp.concatenate([kn_g[l], zl(LANES - NOPE)])[None, :],
        "kngpad": jnp.concatenate([kn_g[l], zl(LANES - NOPE)])[None, :],
        "wuv": w_uv[l].astype(BF16),
        "wukt": w_uk[l].T.astype(BF16),
        "wukt_pad": wukt_pad.astype(BF16),
        "wout": w_out[l].astype(BF16),
    }


def kernel(x_prompt, x_sample, cache_latent, cache_krope, state_pool, state_conv, page_table,
           c_prompt, c_sample, norm_g, w_ada, b_ada, w_in, pool_w, pool_scale, conv_w,
           q_norm_g, w_uq, qn_g, qr_g, kv_norm_g, kr_g, w_uk, kn_g, w_uv, w_out):
    depth = w_in.shape[0]
    bp, seq, d = x_prompt.shape
    db, ts, _ = x_sample.shape
    page = cache_latent.shape[2]
    past_len = page_table.shape[1] * page
    ts_p = 512 if seq % 512 == 0 else seq
    tq = 512 if seq % 512 == 0 else seq
    g_pages = 32
    cache_krope_t = jnp.swapaxes(cache_krope, 2, 3)

    mod = _modulation(jnp.concatenate([c_prompt, c_sample], axis=0), w_ada, b_ada)
    tab_p = _rope_tables(jnp.arange(seq, dtype=jnp.int32))
    tab_s = jnp.repeat(_rope_tables(past_len + jnp.arange(ts, dtype=jnp.int32)), db, axis=1)
    lws = [_layer_weights(l, norm_g, w_in, pool_w, pool_scale, conv_w, q_norm_g, w_uq, qn_g, qr_g,
                          kv_norm_g, kr_g, w_uk, kn_g, w_uv, w_out) for l in range(depth)]

    xp = x_prompt
    lat_p, kr_p, pool_p, conv_p = [], [], [], []
    for l in range(depth):
        mod_p = mod[l, :bp].reshape(bp, 3, d)
        gpc, zmg, qp, kp, v, lat, kr, pst, cst = _prompt_in(xp, mod_p, lws[l], tab_p, ts_p)
        xp = _prompt_out(qp, kp, v, gpc, zmg, xp, mod_p, lws[l]["wout"], tq)
        lat_p.append(lat)
        kr_p.append(kr)
        pool_p.append(pst)
        conv_p.append(cst)

    xs2 = x_sample.reshape(db, ts * d)
    lat_s, kr_s, pool_s, conv_s = [], [], [], []
    for l in range(depth):
        mod_s = mod[l, bp:]
        pst = state_pool[l].reshape(db, -1)
        cst = state_conv[l].reshape(db, -1)
        gpc, zmg, qabs, qr, latn, krn, psto, csto = _sample_in(xs2, mod_s, lws[l], tab_s, pst, cst, ts, past_len)
        nq = N_HEADS * ts
        lat_out = _sample_attn(page_table, qabs.reshape(db, nq, LANES), qr.reshape(db, nq, LANES),
                               latn.reshape(db, ts, LANES), krn.reshape(db, ts, LANES), lws[l]["wukt"],
                               cache_latent, cache_krope_t, l, g_pages, ts)
        xs2 = _sample_out(lat_out.reshape(db, nq * LANES), lws[l]["wuv"], gpc, zmg, xs2, mod_s,
                          lws[l]["wout"], ts)
        lat_s.append(latn.reshape(db, ts, LANES))
        kr_s.append(krn.reshape(db, ts, LANES)[:, :, NOPE:NOPE + ROPE])
        pool_s.append(psto.reshape(state_pool.shape[1:]))
        conv_s.append(csto.reshape(state_conv.shape[1:]))

    return (xp, xs2.reshape(x_sample.shape),
            jnp.stack(lat_p), jnp.stack(kr_p), jnp.stack(pool_p), jnp.stack(conv_p),
            jnp.stack(lat_s), jnp.stack(kr_s), jnp.stack(pool_s), jnp.stack(conv_s))
```

```python
import functools
import math

import jax
import jax.numpy as jnp
from jax import lax
from jax.experimental import pallas as pl
from jax.experimental.pallas import tpu as pltpu

F32 = jnp.float32
BF16 = jnp.bfloat16

POOL_WINDOWS = (2, 4, 8, 16)
POOL_MAX = max(POOL_WINDOWS)
CONV_W = 3
N_HEADS = 4
NOPE = 64
ROPE = 32
ROPE_THETA = 10000.0
EPS = 1e-6
NEG = -1e30
SM_SCALE = 1.0 / math.sqrt(NOPE + ROPE)
LANES = 128
HALO = 16
VMEM_LIMIT = 56 * 1024 * 1024

C_POOL = 0
C_CONV = 512
C_MLA = 1536
C_ZM = 2048
D_IN_PAD = 2560


def _sigmoid(x):
    return 1.0 / (1.0 + jnp.exp(-x))


def _silu(x):
    return x * _sigmoid(x)


def _dot(a, b):
    return jnp.dot(a, b, preferred_element_type=F32)


def _dot_nt(a, b):
    return lax.dot_general(a, b, (((1,), (1,)), ((), ())), preferred_element_type=F32)


def _rms(x, n):
    return x * lax.rsqrt(jnp.sum(x * x, axis=-1, keepdims=True) * (1.0 / n) + EPS)


def _rope_block(x, cos_t, sin_a, sin_b):
    return x * cos_t + pltpu.roll(x, LANES - ROPE // 2, 1) * sin_a + pltpu.roll(x, ROPE // 2, 1) * sin_b


def _mod_kernel(cp_ref, cs_ref, w_ref, b_ref, op_ref, os_ref):
    w = w_ref[0].astype(BF16)
    op_ref[0] = _dot(_silu(cp_ref[...]).astype(BF16), w) + b_ref[0]
    os_ref[0] = _dot(_silu(cs_ref[...]).astype(BF16), w) + b_ref[0]


def _modulation(c_prompt, c_sample, w_ada, b_ada):
    depth, d, d3 = w_ada.shape
    bp, db = c_prompt.shape[0], c_sample.shape[0]
    nb = d3 // d
    return pl.pallas_call(
        _mod_kernel,
        out_shape=[jax.ShapeDtypeStruct((depth, bp, d3), F32), jax.ShapeDtypeStruct((depth, db, d3), F32)],
        grid=(depth, nb),
        in_specs=[
            pl.BlockSpec((bp, d), lambda l, j: (0, 0)),
            pl.BlockSpec((db, d), lambda l, j: (0, 0)),
            pl.BlockSpec((1, d, d), lambda l, j: (l, 0, j)),
            pl.BlockSpec((1, 1, d), lambda l, j: (l, 0, j)),
        ],
        out_specs=[pl.BlockSpec((1, bp, d), lambda l, j: (l, 0, j)),
                   pl.BlockSpec((1, db, d), lambda l, j: (l, 0, j))],
        compiler_params=pltpu.CompilerParams(vmem_limit_bytes=VMEM_LIMIT),
        name="adaln_mod",
    )(c_prompt, c_sample, w_ada, b_ada.reshape(depth, 1, d3))


def _mla_prep(pm, qng, wuq, s128q, gq, kvg, krg, cos_t, sin_a, sin_b):
    cq = pm[:, 0:256]
    ckv_raw = pm[:, 256:384]
    kr_raw = pm[:, 384:512]
    cqn = (_rms(cq, 256) * qng).astype(BF16)
    q = _dot(cqn, wuq)
    qh = []
    for h in range(N_HEADS):
        xh = q[:, h * LANES:(h + 1) * LANES]
        ms = _dot((xh * xh).astype(BF16), s128q)
        xh = xh * lax.rsqrt(ms + EPS) * gq
        qh.append(_rope_block(xh, cos_t, sin_a, sin_b))
    ckv = _rms(ckv_raw, 128) * kvg
    krn = _rms(kr_raw, ROPE) * krg
    kr = _rope_block(krn, cos_t, sin_a, sin_b)
    return qh, ckv, kr


def _prompt_in_kernel(x_ref, mod_ref, g_ref, win_ref, poolw_ref, pscale_ref, convw_ref,
                      qng_ref, wuq_ref, s128q_ref, gq_ref, kvg_ref, krg_ref,
                      wuk_ref, s128k_ref, gk_ref, wuv_ref, tab_ref,
                      gpc_ref, zmg_ref, qp_ref, kp_ref, v_ref, lat_ref, kr_ref, pst_ref, cst_ref,
                      uext, vext, *, ts, nt):
    t = pl.program_id(1)
    x = x_ref[0]
    ms = jnp.mean(x * x, axis=-1, keepdims=True)
    h = (x * lax.rsqrt(ms + EPS) * g_ref[...]) * (1.0 + mod_ref[0, 1:2, :]) + mod_ref[0, 0:1, :]
    hb = h.astype(BF16)

    @pl.when(t == 0)
    def _():
        uext[0:HALO, :] = jnp.zeros((HALO, 256), F32)
        vext[0:HALO, :] = jnp.zeros((HALO, 256), F32)

    pp = _dot(hb, win_ref[:, C_POOL:C_POOL + 512])
    u = pp[:, 0:256]
    zp = pp[:, 256:512]
    uext[HALO:HALO + ts, :] = u
    ext = uext[...]
    a1 = ext + pltpu.roll(ext, 1, 0)
    a2 = a1 + pltpu.roll(a1, 2, 0)
    a3 = a2[:, 128:256]
    a3 = a3 + pltpu.roll(a3, 4, 0)
    a4 = a3 + pltpu.roll(a3, 8, 0)
    pos1 = t * ts + lax.broadcasted_iota(jnp.int32, (ts, 1), 0) + 1
    inv = [1.0 / jnp.minimum(pos1, w).astype(F32) for w in POOL_WINDOWS]
    lane = lax.broadcasted_iota(jnp.int32, (1, LANES), 1)
    mean_l = jnp.where(lane < 64, a1[HALO:, 0:128] * inv[0], a2[HALO:, 0:128] * inv[1])
    mean_r = jnp.where(lane < 64, a3[HALO:] * inv[2], a4[HALO:] * inv[3])
    d = jnp.concatenate([mean_l, mean_r], axis=1) - u
    yp = _dot(d.astype(BF16), poolw_ref[...]) * pscale_ref[...]
    gp = yp * _silu(zp)
    uext[0:HALO, :] = u[ts - HALO:ts, :]

    pc = _dot(hb, win_ref[:, C_CONV:C_CONV + 1024])
    hc = pc[:, 0:256]
    bg = pc[:, 256:512]
    cg = pc[:, 512:768]
    zc = pc[:, 768:1024]
    v = cg * hc
    vext[HALO:HALO + ts, :] = v
    e = vext[...]
    y = convw_ref[0:1, :] * pltpu.roll(e, 2, 0) + convw_ref[1:2, :] * pltpu.roll(e, 1, 0) + convw_ref[2:3, :] * e
    gc = bg * y[HALO:] * _silu(zc)
    gpc_ref[0] = jnp.concatenate([gp, gc], axis=1).astype(BF16)
    vext[0:HALO, :] = v[ts - HALO:ts, :]

    zmg_ref[0] = _silu(_dot(hb, win_ref[:, C_ZM:C_ZM + 512])).astype(BF16)

    pm = _dot(hb, win_ref[:, C_MLA:C_MLA + 512])
    cos_t = tab_ref[0]
    sin_a = tab_ref[1]
    sin_b = tab_ref[2]
    qh, ckv, kr = _mla_prep(pm, qng_ref[...], wuq_ref[...], s128q_ref[...], gq_ref[...],
                            kvg_ref[...], krg_ref[...], cos_t, sin_a, sin_b)
    qp_ref[0] = jnp.concatenate(qh, axis=1).astype(BF16)
    lat_ref[0] = ckv
    kr_ref[0] = kr[:, NOPE:NOPE + ROPE]
    cb = ckv.astype(BF16)
    kc = _dot(cb, wuk_ref[...])
    kh = []
    for hh in range(N_HEADS):
        xh = kc[:, hh * LANES:(hh + 1) * LANES]
        msk = _dot((xh * xh).astype(BF16), s128k_ref[...])
        kh.append(xh * lax.rsqrt(msk + EPS) * gk_ref[...] + kr)
    kp_ref[0] = jnp.concatenate(kh, axis=1).astype(BF16)
    v_ref[0] = _dot(cb, wuv_ref[...]).astype(BF16)

    @pl.when(t == nt - 1)
    def _():
        pst_ref[0] = uext[pl.ds(HALO + ts - (POOL_MAX - 1), POOL_MAX - 1), :]
        cst_ref[0] = vext[pl.ds(HALO + ts - (CONV_W - 1), CONV_W - 1), :]


def _prompt_in(x, mod_p, lw, tab, ts):
    b, t, d = x.shape
    nt = t // ts
    const = lambda shape: pl.BlockSpec(shape, lambda i, j: (0,) * len(shape))
    tile = lambda w: pl.BlockSpec((1, ts, w), lambda i, j: (i, j, 0))
    outs = [
        jax.ShapeDtypeStruct((b, t, 512), BF16),
        jax.ShapeDtypeStruct((b, t, 512), BF16),
        jax.ShapeDtypeStruct((b, t, 512), BF16),
        jax.ShapeDtypeStruct((b, t, 512), BF16),
        jax.ShapeDtypeStruct((b, t, 512), BF16),
        jax.ShapeDtypeStruct((b, t, 128), F32),
        jax.ShapeDtypeStruct((b, t, ROPE), F32),
        jax.ShapeDtypeStruct((b, POOL_MAX - 1, 256), F32),
        jax.ShapeDtypeStruct((b, CONV_W - 1, 256), F32),
    ]
    return pl.pallas_call(
        functools.partial(_prompt_in_kernel, ts=ts, nt=nt),
        out_shape=outs,
        grid=(b, nt),
        in_specs=[
            tile(d),
            pl.BlockSpec((1, 3, d), lambda i, j: (i, 0, 0)),
            const((1, d)),
            const((d, D_IN_PAD)),
            const((256, 256)), const((1, 256)), const((CONV_W, 256)),
            const((1, 256)), const((256, 512)), const((128, 128)), const((1, 128)),
            const((1, 128)), const((1, 128)),
            const((128, 512)), const((128, 128)), const((1, 128)), const((128, 512)),
            pl.BlockSpec((3, ts, 128), lambda i, j: (0, j, 0)),
        ],
        out_specs=[
            tile(512), tile(512), tile(512), tile(512), tile(512), tile(128), tile(ROPE),
            pl.BlockSpec((1, POOL_MAX - 1, 256), lambda i, j: (i, 0, 0)),
            pl.BlockSpec((1, CONV_W - 1, 256), lambda i, j: (i, 0, 0)),
        ],
        scratch_shapes=[pltpu.VMEM((HALO + ts, 256), F32), pltpu.VMEM((HALO + ts, 256), F32)],
        compiler_params=pltpu.CompilerParams(
            dimension_semantics=("parallel", "arbitrary"), vmem_limit_bytes=VMEM_LIMIT),
        name="prompt_in",
    )(x, mod_p, lw["norm_g"], lw["win"], lw["poolw"], lw["pscale"], lw["convw"],
      lw["qng"], lw["wuq"], lw["s128q"], lw["gq"], lw["kvg"], lw["krg"],
      lw["wuk"], lw["s128k"], lw["gk"], lw["wuv"], tab)


def _prompt_out_kernel(q_ref, k_ref, v_ref, gpc_ref, zmg_ref, x_ref, gate_ref, wout_ref, y_ref,
                       s_scr, p_scr, ym_scr, *, tq, nt):
    i = pl.program_id(1)
    row = lax.broadcasted_iota(jnp.int32, (tq, LANES), 0)
    lane = lax.broadcasted_iota(jnp.int32, (tq, LANES), 1)
    n_diag = tq // LANES

    def attend(n_keys):
        n_pieces = n_keys // LANES
        for h in range(N_HEADS):
            par = h % 2
            sl = slice(h * LANES, (h + 1) * LANES)
            s_scr[par, :, 0:n_keys] = _dot_nt(q_ref[0, :, sl], k_ref[0, 0:n_keys, sl])

            def piece(t):
                x = s_scr[par, :, t * LANES:(t + 1) * LANES]
                td = t - (n_pieces - n_diag)
                if td >= 0:
                    x = jnp.where(td * LANES + lane <= row, x, NEG)
                return x

            mx = piece(0)
            for t in range(1, n_pieces):
                mx = jnp.maximum(mx, piece(t))
            m_rep = jnp.broadcast_to(jnp.max(mx, axis=-1, keepdims=True), (tq, LANES))
            psum = jnp.zeros((tq, LANES), F32)
            for t in range(n_pieces):
                p_t = jnp.exp(piece(t) - m_rep)
                psum = psum + p_t
                p_scr[par, :, t * LANES:(t + 1) * LANES] = p_t.astype(BF16)
            o = _dot(p_scr[par, :, 0:n_keys], v_ref[0, 0:n_keys, sl])
            ym_scr[:, sl] = o / jnp.sum(psum, axis=-1, keepdims=True)

    for v in range(nt):
        pl.when(i == v)(functools.partial(attend, (v + 1) * tq))

    ym = ym_scr[...] * zmg_ref[0].astype(F32)
    o = _dot(gpc_ref[0], wout_ref[0:512, :]) + _dot(ym.astype(BF16), wout_ref[512:1024, :])
    y_ref[0] = x_ref[0] + gate_ref[0, 2:3, :] * o


def _prompt_out(qp, kp, v, gpc, zmg, x, mod_p, wout, tq):
    b, t, d = x.shape
    tile = lambda w: pl.BlockSpec((1, tq, w), lambda i, j: (i, j, 0))
    full = lambda w: pl.BlockSpec((1, t, w), lambda i, j: (i, 0, 0))
    return pl.pallas_call(
        functools.partial(_prompt_out_kernel, tq=tq, nt=t // tq),
        out_shape=jax.ShapeDtypeStruct((b, t, d), F32),
        grid=(b, t // tq),
        in_specs=[
            tile(512), full(512), full(512), tile(512), tile(512), tile(d),
            pl.BlockSpec((1, 3, d), lambda i, j: (i, 0, 0)),
            pl.BlockSpec((d, d), lambda i, j: (0, 0)),
        ],
        out_specs=tile(d),
        scratch_shapes=[pltpu.VMEM((2, tq, t), F32), pltpu.VMEM((2, tq, t), BF16),
                        pltpu.VMEM((tq, N_HEADS * LANES), F32)],
        compiler_params=pltpu.CompilerParams(
            dimension_semantics=("parallel", "arbitrary"), vmem_limit_bytes=VMEM_LIMIT),
        name="prompt_out",
    )(qp, kp, v, gpc, zmg, x, mod_p, wout)


def _sample_in_kernel(x_ref, mod_ref, g_ref, win_ref, poolw_ref, pscale_ref, convw_ref,
                      qng_ref, wuq_ref, s128q_ref, gq_ref, kvg_ref, krg_ref, kng_ref, wukt_ref, tab_ref,
                      pst_ref, cst_ref,
                      gpc_ref, zmg_ref, qabs_ref, qr_ref, lat_ref, krp_ref, psto_ref, csto_ref,
                      *, db, ts, d, past_len):
    shift = mod_ref[:, 0:d]
    scale1 = 1.0 + mod_ref[:, d:2 * d]
    hs = []
    for t in range(ts):
        x = x_ref[:, t * d:(t + 1) * d]
        ms = jnp.mean(x * x, axis=-1, keepdims=True)
        hs.append(((x * lax.rsqrt(ms + EPS) * g_ref[...]) * scale1 + shift).astype(BF16))
    hb = jnp.concatenate(hs, axis=0)
    rows = lambda a, t: a[t * db:(t + 1) * db]

    pp = _dot(hb, win_ref[:, C_POOL:C_POOL + 512])
    zp = pp[:, 256:512]
    np_ = POOL_MAX - 1
    ext = [pst_ref[k] for k in range(np_)] + [rows(pp, t)[:, 0:256] for t in range(ts)]
    for k in range(np_):
        psto_ref[k] = ext[ts + k]

    def doubled(prev, step, lo):
        return {r: prev[r] + prev[r - step] for r in range(lo, np_ + ts) if r in prev and (r - step) in prev}

    a0 = dict(enumerate(ext))
    a1 = doubled(a0, 1, 1)
    a2 = doubled(a1, 2, 3)
    a3 = doubled(a2, 4, 7)
    a4 = doubled(a3, 8, 15)
    lane = lax.broadcasted_iota(jnp.int32, (1, 256), 1)
    ds_ = []
    for t in range(ts):
        r = np_ + t
        inv = [1.0 / float(min(past_len + t + 1, w)) for w in POOL_WINDOWS]
        mean = jnp.where(lane < 64, a1[r] * inv[0],
                         jnp.where(lane < 128, a2[r] * inv[1],
                                   jnp.where(lane < 192, a3[r] * inv[2], a4[r] * inv[3])))
        ds_.append(mean - ext[r])
    dcat = jnp.concatenate(ds_, axis=0)
    gp = _dot(dcat.astype(BF16), poolw_ref[...]) * pscale_ref[...] * _silu(zp)

    pc = _dot(hb, win_ref[:, C_CONV:C_CONV + 1024])
    bg = pc[:, 256:512]
    zc = pc[:, 768:1024]
    vfull = pc[:, 512:768] * pc[:, 0:256]
    nc = CONV_W - 1
    vx = [cst_ref[:, k * 256:(k + 1) * 256] for k in range(nc)] + [rows(vfull, t) for t in range(ts)]
    for k in range(nc):
        csto_ref[:, k * 256:(k + 1) * 256] = vx[ts + k]
    ys = []
    for t in range(ts):
        acc = convw_ref[0:1, :] * vx[t]
        for k in range(1, CONV_W):
            acc = acc + convw_ref[k:k + 1, :] * vx[t + k]
        ys.append(acc)
    gc = bg * jnp.concatenate(ys, axis=0) * _silu(zc)
    gpc_ref[...] = jnp.concatenate([gp, gc], axis=1).astype(BF16)

    zmg_ref[...] = _silu(_dot(hb, win_ref[:, C_ZM:C_ZM + 512])).astype(BF16)

    pm = _dot(hb, win_ref[:, C_MLA:C_MLA + 512])
    qh, ckv, kr = _mla_prep(pm, qng_ref[...], wuq_ref[...], s128q_ref[...], gq_ref[...],
                            kvg_ref[...], krg_ref[...], tab_ref[0], tab_ref[1], tab_ref[2])
    for t in range(ts):
        lat_ref[:, t * LANES:(t + 1) * LANES] = rows(ckv, t)
        krp_ref[:, t * LANES:(t + 1) * LANES] = rows(kr, t)
    for h in range(N_HEADS):
        qa = _dot((qh[h] * kng_ref[...]).astype(BF16), wukt_ref[h])
        for t in range(ts):
            c0 = (h * ts + t) * LANES
            qabs_ref[:, c0:c0 + LANES] = rows(qa, t).astype(BF16)
            qr_ref[:, c0:c0 + LANES] = rows(qh[h], t).astype(BF16)


def _sample_in(xs2, mod_s, lw, tab, pst, cst, ts, past_len):
    db = xs2.shape[0]
    d = xs2.shape[1] // ts
    r = ts * db
    outs = [
        jax.ShapeDtypeStruct((r, 512), BF16),
        jax.ShapeDtypeStruct((r, 512), BF16),
        jax.ShapeDtypeStruct((db, N_HEADS * ts * LANES), BF16),
        jax.ShapeDtypeStruct((db, N_HEADS * ts * LANES), BF16),
        jax.ShapeDtypeStruct((db, ts * LANES), F32),
        jax.ShapeDtypeStruct((db, ts * LANES), F32),
        jax.ShapeDtypeStruct(pst.shape, F32),
        jax.ShapeDtypeStruct(cst.shape, F32),
    ]
    return pl.pallas_call(
        functools.partial(_sample_in_kernel, db=db, ts=ts, d=d, past_len=past_len),
        out_shape=outs,
        compiler_params=pltpu.CompilerParams(vmem_limit_bytes=VMEM_LIMIT),
        name="sample_in",
    )(xs2, mod_s, lw["norm_g"], lw["win"], lw["poolw"], lw["pscale"], lw["convw"],
      lw["qng"], lw["wuq"], lw["s128q"], lw["gq"], lw["kvg"], lw["krg"], lw["kngpad"], lw["wukt_pad"], tab,
      pst, cst)


def _sample_attn_kernel(pt_ref, qabs_ref, qr_ref, latn_ref, krn_ref, wukt_ref, lat_hbm, krt_hbm, out_ref,
                        lhs, newlat, newkr, lat_buf, krt_buf, latb, s_scr, mx_scr, sem, m_s, l_s, acc_s,
                        *, layer, g, ts, page, n_pages):
    b = pl.program_id(0)
    nb = pl.num_programs(0)
    nc = n_pages // g
    nq = N_HEADS * ts
    nk = N_HEADS * NOPE
    half = (b % 2) * nc
    other = nc - half

    def page_copies(seq, chunk, slot, j):
        pg = pt_ref[seq * n_pages + chunk * g + j]
        return (pltpu.make_async_copy(lat_hbm.at[layer, pg], lat_buf.at[slot, j], sem.at[0, slot]),
                pltpu.make_async_copy(krt_hbm.at[layer, pg], krt_buf.at[slot, j], sem.at[1, slot]))

    def wait_chunk(slot):
        pltpu.make_async_copy(lat_hbm.at[layer, pl.ds(0, g)], lat_buf.at[slot], sem.at[0, slot]).wait()
        pltpu.make_async_copy(krt_hbm.at[layer, pl.ds(0, g)], krt_buf.at[slot], sem.at[1, slot]).wait()

    def scores(lat_bf, s_rope, lat_t_bf=None):
        res = _dot_nt(lhs[...], lat_bf) if lat_t_bf is None else _dot(lhs[...], lat_t_bf)
        kt = res[0:nk]
        k2 = kt * kt
        rr = []
        for h in range(N_HEADS):
            ss = jnp.sum(k2[h * NOPE:(h + 1) * NOPE], axis=0, keepdims=True)
            rr.append(jnp.broadcast_to(lax.rsqrt(ss * (1.0 / NOPE) + EPS), (ts, lat_bf.shape[0])))
        return res[nk:nk + nq] * jnp.concatenate(rr, axis=0) + s_rope

    def tail(slot):
        m_old = m_s[...]
        m_new = jnp.maximum(m_old, mx_scr[slot])
        corr = jnp.exp(m_old - m_new)
        psum = jnp.zeros((nq, LANES), F32)
        ps = []
        for t in range(g * page // LANES):
            p_t = jnp.exp(s_scr[slot, :, t * LANES:(t + 1) * LANES] - m_new)
            psum = psum + p_t
            ps.append(p_t.astype(BF16))
        l_s[...] = l_s[...] * corr + psum
        acc_s[...] = acc_s[...] * corr + _dot(jnp.concatenate(ps, axis=1), latb[slot])
        m_s[...] = m_new

    @pl.when(b == 0)
    def _():
        for c in range(nc):
            for j in range(g):
                for cp in page_copies(0, c, c, j):
                    cp.start()
        lhs[0:nk, :] = wukt_ref[...]
        newlat[...] = jnp.zeros((page, LANES), F32)
        newkr[...] = jnp.zeros((page, ROPE), F32)
        latb[1] = jnp.zeros((g * page, LANES), BF16)

    wait_chunk(half)
    wait_chunk(half + 1)

    qr_bf = qr_ref[0].astype(F32)[:, NOPE:NOPE + ROPE].astype(BF16)
    lhs[nk:nk + nq, :] = qabs_ref[0]
    newlat[0:ts, :] = latn_ref[0]
    newkr[0:ts, :] = krn_ref[0][:, NOPE:NOPE + ROPE]
    m_s[...] = jnp.full((nq, LANES), NEG, F32)
    l_s[...] = jnp.zeros((nq, LANES), F32)
    acc_s[...] = jnp.zeros((nq, LANES), F32)
    lat_bf = newlat[...].astype(BF16)
    s = scores(lat_bf, _dot_nt(qr_bf, newkr[...].astype(BF16)))
    key = lax.broadcasted_iota(jnp.int32, (nq, page), 1)
    qi = lax.broadcasted_iota(jnp.int32, (nq, page), 0) % ts
    s = jnp.where(key <= qi, s, NEG)
    s_scr[1] = jnp.full((nq, g * page), NEG, F32)
    s_scr[1, :, 0:page] = s
    mx_scr[1] = jnp.broadcast_to(jnp.max(s, axis=-1, keepdims=True), (nq, LANES))
    latb[1, 0:page, :] = lat_bf

    def head(c):
        par = c % 2
        parts = []
        for jj in range(g // 2):
            for j in (2 * jj, 2 * jj + 1):
                @pl.when(b + 1 < nb)
                def _(j=j):
                    for cp in page_copies(b + 1, c, other + c, j):
                        cp.start()
            lat_f = lat_buf[half + c, 2 * jj:2 * jj + 2].reshape(2 * page, LANES)
            lat_bf = lat_f.astype(BF16)
            latb[par, 2 * jj * page:(2 * jj + 2) * page, :] = lat_bf
            kr2 = jnp.concatenate([krt_buf[half + c, 2 * jj], krt_buf[half + c, 2 * jj + 1]],
                                  axis=1).astype(BF16)
            parts.append(scores(lat_bf, _dot(qr_bf, kr2), lat_f.T.astype(BF16)))
        s_scr[par] = jnp.concatenate(parts, axis=1)
        mx = parts[0]
        for part in parts[1:]:
            mx = jnp.maximum(mx, part)
        mx_scr[par] = jnp.broadcast_to(jnp.max(mx, axis=-1, keepdims=True), (nq, LANES))

    for c in range(nc):
        if c % 2 == 0 and c > 0:
            wait_chunk(half + c)
            wait_chunk(half + c + 1)
        tail(1 - c % 2)
        head(c)
    tail((nc - 1) % 2)
    out_ref[0] = acc_s[...] / jnp.sum(l_s[...], axis=-1, keepdims=True)


def _sample_attn(page_table, qabs, qr, latn, krn, wukt, cache_latent, cache_krope_t, layer, g, ts):
    db, n_pages = page_table.shape
    page = cache_latent.shape[2]
    nc = n_pages // g
    assert n_pages % g == 0 and nc % 2 == 0
    nq = N_HEADS * ts
    per_seq = lambda rows, w: pl.BlockSpec((1, rows, w), lambda b, pt: (b, 0, 0))
    grid_spec = pltpu.PrefetchScalarGridSpec(
        num_scalar_prefetch=1,
        grid=(db,),
        in_specs=[per_seq(nq, LANES), per_seq(nq, LANES), per_seq(ts, LANES), per_seq(ts, LANES),
                  pl.BlockSpec((N_HEADS * NOPE, LANES), lambda b, pt: (0, 0)),
                  pl.BlockSpec(memory_space=pl.ANY), pl.BlockSpec(memory_space=pl.ANY)],
        out_specs=per_seq(nq, LANES),
        scratch_shapes=[
            pltpu.VMEM((N_HEADS * NOPE + nq, LANES), BF16),
            pltpu.VMEM((page, LANES), F32), pltpu.VMEM((page, ROPE), F32),
            pltpu.VMEM((2 * nc, g, page, LANES), F32),
            pltpu.VMEM((2 * nc, g, ROPE, page), F32),
            pltpu.VMEM((2, g * page, LANES), BF16),
            pltpu.VMEM((2, nq, g * page), F32),
            pltpu.VMEM((2, nq, LANES), F32),
            pltpu.SemaphoreType.DMA((2, 2 * nc)),
            pltpu.VMEM((nq, LANES), F32), pltpu.VMEM((nq, LANES), F32), pltpu.VMEM((nq, LANES), F32),
        ],
    )
    return pl.pallas_call(
        functools.partial(_sample_attn_kernel, layer=layer, g=g, ts=ts, page=page, n_pages=n_pages),
        out_shape=jax.ShapeDtypeStruct((db, nq, LANES), F32),
        grid_spec=grid_spec,
        compiler_params=pltpu.CompilerParams(
            dimension_semantics=("arbitrary",), vmem_limit_bytes=VMEM_LIMIT),
        name="sample_attn",
    )(page_table.reshape(-1), qabs, qr, latn, krn, wukt, cache_latent, cache_krope_t)


def _sample_out_kernel(lo_ref, wuv_ref, gpc_ref, zmg_ref, x_ref, mod_ref, wout_ref, y_ref, *, db, ts, d):
    gate = mod_ref[:, 2 * d:3 * d]
    for t in range(ts):
        ym = jnp.concatenate(
            [_dot(lo_ref[:, (h * ts + t) * LANES:(h * ts + t + 1) * LANES].astype(BF16),
                  wuv_ref[:, h * LANES:(h + 1) * LANES]) for h in range(N_HEADS)], axis=1)
        ymg = (ym * zmg_ref[t * db:(t + 1) * db, :].astype(F32)).astype(BF16)
        o = _dot(gpc_ref[t * db:(t + 1) * db, :], wout_ref[0:512, :]) + _dot(ymg, wout_ref[512:1024, :])
        y_ref[:, t * d:(t + 1) * d] = x_ref[:, t * d:(t + 1) * d] + gate * o


def _sample_out(lat_out2, wuv, gpc, zmg, xs2, mod_s, wout, ts):
    db = xs2.shape[0]
    d = xs2.shape[1] // ts
    return pl.pallas_call(
        functools.partial(_sample_out_kernel, db=db, ts=ts, d=d),
        out_shape=jax.ShapeDtypeStruct(xs2.shape, F32),
        compiler_params=pltpu.CompilerParams(vmem_limit_bytes=VMEM_LIMIT),
        name="sample_out",
    )(lat_out2, wuv, gpc, zmg, xs2, mod_s, wout)


def _rope_tables(pos):
    inv_freq = 1.0 / (ROPE_THETA ** (jnp.arange(0, ROPE, 2, dtype=F32) / ROPE))
    ang = pos.astype(F32)[:, None] * inv_freq[None, :]
    cos, sin = jnp.cos(ang), jnp.sin(ang)
    n = pos.shape[0]
    half = ROPE // 2
    z = lambda w: jnp.zeros((n, w), F32)
    cos_t = jnp.concatenate([jnp.ones((n, NOPE), F32), cos, cos, jnp.ones((n, LANES - NOPE - ROPE), F32)], axis=1)
    sin_a = jnp.concatenate([z(NOPE), -sin, z(LANES - NOPE - half)], axis=1)
    sin_b = jnp.concatenate([z(NOPE + half), sin, z(LANES - NOPE - ROPE)], axis=1)
    return jnp.stack([cos_t, sin_a, sin_b])


def _layer_weights(l, norm_g, w_in, pool_w, pool_scale, conv_w, q_norm_g, w_uq, qn_g, qr_g,
                   kv_norm_g, kr_g, w_uk, kn_g, w_uv, w_out):
    d = w_in.shape[1]
    wi = w_in[l].astype(BF16)
    zc = lambda w: jnp.zeros((d, w), BF16)
    win = jnp.concatenate([wi[:, 0:1920], zc(NOPE), wi[:, 1920:1952], zc(LANES - NOPE - ROPE), wi[:, 1952:]], axis=1)
    pad_head = lambda w, per: jnp.concatenate(
        [jnp.concatenate([w[:, h * per:(h + 1) * per], jnp.zeros((w.shape[0], LANES - per), F32)], axis=1)
         for h in range(N_HEADS)], axis=1)
    seg = jnp.arange(LANES)
    nope_m = (seg < NOPE).astype(F32)
    rope_m = ((seg >= NOPE) & (seg < NOPE + ROPE)).astype(F32)
    s128k = nope_m[:, None] * nope_m[None, :] / NOPE
    s128q = s128k + rope_m[:, None] * rope_m[None, :] / ROPE
    zl = lambda w: jnp.zeros((w,), F32)
    pw = pool_w[l]
    gw = pw.shape[1]
    poolw = jnp.zeros((256, 256), F32)
    for gi in range(pw.shape[0]):
        poolw = poolw.at[gi * gw:(gi + 1) * gw, gi * gw:(gi + 1) * gw].set(pw[gi])
    wukt_pad = jnp.stack([jnp.concatenate([w_uk[l][:, h * NOPE:(h + 1) * NOPE].T,
                                           jnp.zeros((LANES - NOPE, w_uk.shape[1]), F32)], axis=0)
                          for h in range(N_HEADS)])
    return {
        "norm_g": norm_g[l][None, :],
        "win": win,
        "poolw": poolw.astype(BF16),
        "pscale": pool_scale[l][None, :],
        "convw": conv_w[l],
        "qng": q_norm_g[l][None, :],
        "wuq": pad_head(w_uq[l], NOPE + ROPE).astype(BF16),
        "s128q": s128q.astype(BF16),
        "s128k": s128k.astype(BF16),
        "gq": (jnp.concatenate([qn_g[l], qr_g[l], zl(LANES - NOPE - ROPE)]) * SM_SCALE)[None, :],
        "kvg": kv_norm_g[l][None, :],
        "krg": jnp.concatenate([zl(NOPE), kr_g[l], zl(LANES - NOPE - ROPE)])[None, :],
        "wuk": pad_head(w_uk[l], NOPE).astype(BF16),
        "gk": jnp.concatenate([kn_g[l], zl(LANES - NOPE)])[None, :],
        "kngpad": jnp.concatenate([kn_g[l], zl(LANES - NOPE)])[None, :],
        "wuv": w_uv[l].astype(BF16),
        "wukt": w_uk[l].T.astype(BF16),
        "wukt_pad": wukt_pad.astype(BF16),
        "wout": w_out[l].astype(BF16),
    }


def kernel(x_prompt, x_sample, cache_latent, cache_krope, state_pool, state_conv, page_table,
           c_prompt, c_sample, norm_g, w_ada, b_ada, w_in, pool_w, pool_scale, conv_w,
           q_norm_g, w_uq, qn_g, qr_g, kv_norm_g, kr_g, w_uk, kn_g, w_uv, w_out):
    depth = w_in.shape[0]
    bp, seq, d = x_prompt.shape
    db, ts, _ = x_sample.shape
    page = cache_latent.shape[2]
    past_len = page_table.shape[1] * page
    ts_p = 512 if seq % 512 == 0 else seq
    tq = 512 if seq % 512 == 0 else seq
    g_pages = 32
    cache_krope_t = jnp.swapaxes(cache_krope, 2, 3)

    mod_prompt, mod_sample = _modulation(c_prompt, c_sample, w_ada, b_ada)
    tab_p = _rope_tables(jnp.arange(seq, dtype=jnp.int32))
    tab_s = jnp.repeat(_rope_tables(past_len + jnp.arange(ts, dtype=jnp.int32)), db, axis=1)
    lws = [_layer_weights(l, norm_g, w_in, pool_w, pool_scale, conv_w, q_norm_g, w_uq, qn_g, qr_g,
                          kv_norm_g, kr_g, w_uk, kn_g, w_uv, w_out) for l in range(depth)]

    xp = x_prompt
    lat_p, kr_p, pool_p, conv_p = [], [], [], []
    for l in range(depth):
        mod_p = mod_prompt[l].reshape(bp, 3, d)
        gpc, zmg, qp, kp, v, lat, kr, pst, cst = _prompt_in(xp, mod_p, lws[l], tab_p, ts_p)
        xp = _prompt_out(qp, kp, v, gpc, zmg, xp, mod_p, lws[l]["wout"], tq)
        lat_p.append(lat)
        kr_p.append(kr)
        pool_p.append(pst)
        conv_p.append(cst)

    xs2 = x_sample.reshape(db, ts * d)
    lat_s, kr_s, pool_s, conv_s = [], [], [], []
    for l in range(depth):
        mod_s = mod_sample[l]
        pst = jnp.swapaxes(state_pool[l], 0, 1)
        cst = state_conv[l].reshape(db, -1)
        gpc, zmg, qabs, qr, latn, krn, psto, csto = _sample_in(xs2, mod_s, lws[l], tab_s, pst, cst, ts, past_len)
        nq = N_HEADS * ts
        lat_out = _sample_attn(page_table, qabs.reshape(db, nq, LANES), qr.reshape(db, nq, LANES),
                               latn.reshape(db, ts, LANES), krn.reshape(db, ts, LANES), lws[l]["wukt"],
                               cache_latent, cache_krope_t, l, g_pages, ts)
        xs2 = _sample_out(lat_out.reshape(db, nq * LANES), lws[l]["wuv"], gpc, zmg, xs2, mod_s,
                          lws[l]["wout"], ts)
        lat_s.append(latn.reshape(db, ts, LANES))
        kr_s.append(krn.reshape(db, ts, LANES)[:, :, NOPE:NOPE + ROPE])
        pool_s.append(jnp.swapaxes(psto, 0, 1))
        conv_s.append(csto.reshape(state_conv.shape[1:]))

    return (xp, xs2.reshape(x_sample.shape),
            jnp.stack(lat_p), jnp.stack(kr_p), jnp.stack(pool_p), jnp.stack(conv_p),
            jnp.stack(lat_s), jnp.stack(kr_s), jnp.stack(pool_s), jnp.stack(conv_s))
```

```python
import functools
import math

import jax
import jax.numpy as jnp
from jax import lax
from jax.experimental import pallas as pl
from jax.experimental.pallas import tpu as pltpu

F32 = jnp.float32
BF16 = jnp.bfloat16

POOL_WINDOWS = (2, 4, 8, 16)
POOL_MAX = max(POOL_WINDOWS)
CONV_W = 3
N_HEADS = 4
NOPE = 64
ROPE = 32
ROPE_THETA = 10000.0
EPS = 1e-6
NEG = -1e30
SM_SCALE = 1.0 / math.sqrt(NOPE + ROPE)
LANES = 128
HALO = 16
VMEM_LIMIT = 56 * 1024 * 1024

C_POOL = 0
C_CONV = 512
C_MLA = 1536
C_ZM = 2048
D_IN_PAD = 2560


def _sigmoid(x):
    return 1.0 / (1.0 + jnp.exp(-x))


def _silu(x):
    return x * _sigmoid(x)


def _dot(a, b):
    return jnp.dot(a, b, preferred_element_type=F32)


def _dot_nt(a, b):
    return lax.dot_general(a, b, (((1,), (1,)), ((), ())), preferred_element_type=F32)


def _rms(x, n):
    return x * lax.rsqrt(jnp.sum(x * x, axis=-1, keepdims=True) * (1.0 / n) + EPS)


def _rope_block(x, cos_t, sin_a, sin_b):
    return x * cos_t + pltpu.roll(x, LANES - ROPE // 2, 1) * sin_a + pltpu.roll(x, ROPE // 2, 1) * sin_b


def _mod_kernel(cp_ref, cs_ref, w_ref, b_ref, op_ref, os_ref):
    w = w_ref[0].astype(BF16)
    op_ref[0] = _dot(_silu(cp_ref[...]).astype(BF16), w) + b_ref[0]
    os_ref[0] = _dot(_silu(cs_ref[...]).astype(BF16), w) + b_ref[0]


def _modulation(c_prompt, c_sample, w_ada, b_ada):
    depth, d, d3 = w_ada.shape
    bp, db = c_prompt.shape[0], c_sample.shape[0]
    nb = d3 // d
    return pl.pallas_call(
        _mod_kernel,
        out_shape=[jax.ShapeDtypeStruct((depth, bp, d3), F32), jax.ShapeDtypeStruct((depth, db, d3), F32)],
        grid=(depth, nb),
        in_specs=[
            pl.BlockSpec((bp, d), lambda l, j: (0, 0)),
            pl.BlockSpec((db, d), lambda l, j: (0, 0)),
            pl.BlockSpec((1, d, d), lambda l, j: (l, 0, j)),
            pl.BlockSpec((1, 1, d), lambda l, j: (l, 0, j)),
        ],
        out_specs=[pl.BlockSpec((1, bp, d), lambda l, j: (l, 0, j)),
                   pl.BlockSpec((1, db, d), lambda l, j: (l, 0, j))],
        compiler_params=pltpu.CompilerParams(vmem_limit_bytes=VMEM_LIMIT),
        name="adaln_mod",
    )(c_prompt, c_sample, w_ada, b_ada.reshape(depth, 1, d3))


def _mla_prep(pm, qng, wuq, s128q, gq, kvg, krg, cos_t, sin_a, sin_b):
    cq = pm[:, 0:256]
    ckv_raw = pm[:, 256:384]
    kr_raw = pm[:, 384:512]
    cqn = (_rms(cq, 256) * qng).astype(BF16)
    q = _dot(cqn, wuq)
    qh = []
    for h in range(N_HEADS):
        xh = q[:, h * LANES:(h + 1) * LANES]
        ms = _dot((xh * xh).astype(BF16), s128q)
        xh = xh * lax.rsqrt(ms + EPS) * gq
        qh.append(_rope_block(xh, cos_t, sin_a, sin_b))
    ckv = _rms(ckv_raw, 128) * kvg
    krn = _rms(kr_raw, ROPE) * krg
    kr = _rope_block(krn, cos_t, sin_a, sin_b)
    return qh, ckv, kr


def _prompt_in_kernel(x_ref, mod_ref, g_ref, win_ref, poolw_ref, pscale_ref, convw_ref,
                      qng_ref, wuq_ref, s128q_ref, gq_ref, kvg_ref, krg_ref,
                      wuk_ref, s128k_ref, gk_ref, wuv_ref, tab_ref,
                      gpc_ref, zmg_ref, qp_ref, kp_ref, v_ref, lat_ref, kr_ref, pst_ref, cst_ref,
                      uext, vext, *, ts, nt):
    t = pl.program_id(1)
    x = x_ref[0]
    ms = jnp.mean(x * x, axis=-1, keepdims=True)
    h = (x * lax.rsqrt(ms + EPS) * g_ref[...]) * (1.0 + mod_ref[0, 1:2, :]) + mod_ref[0, 0:1, :]
    hb = h.astype(BF16)

    @pl.when(t == 0)
    def _():
        uext[0:HALO, :] = jnp.zeros((HALO, 256), F32)
        vext[0:HALO, :] = jnp.zeros((HALO, 256), F32)

    pp = _dot(hb, win_ref[:, C_POOL:C_POOL + 512])
    u = pp[:, 0:256]
    zp = pp[:, 256:512]
    uext[HALO:HALO + ts, :] = u
    ext = uext[...]
    a1 = ext + pltpu.roll(ext, 1, 0)
    a2 = a1 + pltpu.roll(a1, 2, 0)
    a3 = a2[:, 128:256]
    a3 = a3 + pltpu.roll(a3, 4, 0)
    a4 = a3 + pltpu.roll(a3, 8, 0)
    pos1 = t * ts + lax.broadcasted_iota(jnp.int32, (ts, 1), 0) + 1
    inv = [1.0 / jnp.minimum(pos1, w).astype(F32) for w in POOL_WINDOWS]
    lane = lax.broadcasted_iota(jnp.int32, (1, LANES), 1)
    mean_l = jnp.where(lane < 64, a1[HALO:, 0:128] * inv[0], a2[HALO:, 0:128] * inv[1])
    mean_r = jnp.where(lane < 64, a3[HALO:] * inv[2], a4[HALO:] * inv[3])
    d = jnp.concatenate([mean_l, mean_r], axis=1) - u
    yp = _dot(d.astype(BF16), poolw_ref[...]) * pscale_ref[...]
    gp = yp * _silu(zp)
    uext[0:HALO, :] = u[ts - HALO:ts, :]

    pc = _dot(hb, win_ref[:, C_CONV:C_CONV + 1024])
    hc = pc[:, 0:256]
    bg = pc[:, 256:512]
    cg = pc[:, 512:768]
    zc = pc[:, 768:1024]
    v = cg * hc
    vext[HALO:HALO + ts, :] = v
    e = vext[...]
    y = convw_ref[0:1, :] * pltpu.roll(e, 2, 0) + convw_ref[1:2, :] * pltpu.roll(e, 1, 0) + convw_ref[2:3, :] * e
    gc = bg * y[HALO:] * _silu(zc)
    gpc_ref[0] = jnp.concatenate([gp, gc], axis=1).astype(BF16)
    vext[0:HALO, :] = v[ts - HALO:ts, :]

    zmg_ref[0] = _silu(_dot(hb, win_ref[:, C_ZM:C_ZM + 512])).astype(BF16)

    pm = _dot(hb, win_ref[:, C_MLA:C_MLA + 512])
    cos_t = tab_ref[0]
    sin_a = tab_ref[1]
    sin_b = tab_ref[2]
    qh, ckv, kr = _mla_prep(pm, qng_ref[...], wuq_ref[...], s128q_ref[...], gq_ref[...],
                            kvg_ref[...], krg_ref[...], cos_t, sin_a, sin_b)
    qp_ref[0] = jnp.concatenate(qh, axis=1).astype(BF16)
    lat_ref[0] = ckv
    kr_ref[0] = kr[:, NOPE:NOPE + ROPE]
    cb = ckv.astype(BF16)
    kc = _dot(cb, wuk_ref[...])
    kh = []
    for hh in range(N_HEADS):
        xh = kc[:, hh * LANES:(hh + 1) * LANES]
        msk = _dot((xh * xh).astype(BF16), s128k_ref[...])
        kh.append(xh * lax.rsqrt(msk + EPS) * gk_ref[...] + kr)
    kp_ref[0] = jnp.concatenate(kh, axis=1).astype(BF16)
    v_ref[0] = _dot(cb, wuv_ref[...]).astype(BF16)

    @pl.when(t == nt - 1)
    def _():
        pst_ref[0] = uext[pl.ds(HALO + ts - (POOL_MAX - 1), POOL_MAX - 1), :]
        cst_ref[0] = vext[pl.ds(HALO + ts - (CONV_W - 1), CONV_W - 1), :]


def _prompt_in(x, mod_p, lw, tab, ts):
    b, t, d = x.shape
    nt = t // ts
    const = lambda shape: pl.BlockSpec(shape, lambda i, j: (0,) * len(shape))
    tile = lambda w: pl.BlockSpec((1, ts, w), lambda i, j: (i, j, 0))
    outs = [
        jax.ShapeDtypeStruct((b, t, 512), BF16),
        jax.ShapeDtypeStruct((b, t, 512), BF16),
        jax.ShapeDtypeStruct((b, t, 512), BF16),
        jax.ShapeDtypeStruct((b, t, 512), BF16),
        jax.ShapeDtypeStruct((b, t, 512), BF16),
        jax.ShapeDtypeStruct((b, t, 128), F32),
        jax.ShapeDtypeStruct((b, t, ROPE), F32),
        jax.ShapeDtypeStruct((b, POOL_MAX - 1, 256), F32),
        jax.ShapeDtypeStruct((b, CONV_W - 1, 256), F32),
    ]
    return pl.pallas_call(
        functools.partial(_prompt_in_kernel, ts=ts, nt=nt),
        out_shape=outs,
        grid=(b, nt),
        in_specs=[
            tile(d),
            pl.BlockSpec((1, 3, d), lambda i, j: (i, 0, 0)),
            const((1, d)),
            const((d, D_IN_PAD)),
            const((256, 256)), const((1, 256)), const((CONV_W, 256)),
            const((1, 256)), const((256, 512)), const((128, 128)), const((1, 128)),
            const((1, 128)), const((1, 128)),
            const((128, 512)), const((128, 128)), const((1, 128)), const((128, 512)),
            pl.BlockSpec((3, ts, 128), lambda i, j: (0, j, 0)),
        ],
        out_specs=[
            tile(512), tile(512), tile(512), tile(512), tile(512), tile(128), tile(ROPE),
            pl.BlockSpec((1, POOL_MAX - 1, 256), lambda i, j: (i, 0, 0)),
            pl.BlockSpec((1, CONV_W - 1, 256), lambda i, j: (i, 0, 0)),
        ],
        scratch_shapes=[pltpu.VMEM((HALO + ts, 256), F32), pltpu.VMEM((HALO + ts, 256), F32)],
        compiler_params=pltpu.CompilerParams(
            dimension_semantics=("parallel", "arbitrary"), vmem_limit_bytes=VMEM_LIMIT),
        name="prompt_in",
    )(x, mod_p, lw["norm_g"], lw["win"], lw["poolw"], lw["pscale"], lw["convw"],
      lw["qng"], lw["wuq"], lw["s128q"], lw["gq"], lw["kvg"], lw["krg"],
      lw["wuk"], lw["s128k"], lw["gk"], lw["wuv"], tab)


def _prompt_out_kernel(q_ref, k_ref, v_ref, gpc_ref, zmg_ref, x_ref, gate_ref, wout_ref, y_ref,
                       s_scr, p_scr, ym_scr, *, tq, nt):
    i = pl.program_id(1)
    row = lax.broadcasted_iota(jnp.int32, (tq, LANES), 0)
    lane = lax.broadcasted_iota(jnp.int32, (tq, LANES), 1)
    n_diag = tq // LANES

    def attend(n_keys):
        n_pieces = n_keys // LANES
        for h in range(N_HEADS):
            par = h % 2
            sl = slice(h * LANES, (h + 1) * LANES)
            s_scr[par, :, 0:n_keys] = _dot_nt(q_ref[0, :, sl], k_ref[0, 0:n_keys, sl])

            def piece(t):
                x = s_scr[par, :, t * LANES:(t + 1) * LANES]
                td = t - (n_pieces - n_diag)
                if td >= 0:
                    x = jnp.where(td * LANES + lane <= row, x, NEG)
                return x

            mx = piece(0)
            for t in range(1, n_pieces):
                mx = jnp.maximum(mx, piece(t))
            m_rep = jnp.broadcast_to(jnp.max(mx, axis=-1, keepdims=True), (tq, LANES))
            psum = jnp.zeros((tq, LANES), F32)
            for t in range(n_pieces):
                p_t = jnp.exp(piece(t) - m_rep)
                psum = psum + p_t
                p_scr[par, :, t * LANES:(t + 1) * LANES] = p_t.astype(BF16)
            o = _dot(p_scr[par, :, 0:n_keys], v_ref[0, 0:n_keys, sl])
            ym_scr[:, sl] = o / jnp.sum(psum, axis=-1, keepdims=True)

    for v in range(nt):
        pl.when(i == v)(functools.partial(attend, (v + 1) * tq))

    ym = ym_scr[...] * zmg_ref[0].astype(F32)
    o = _dot(gpc_ref[0], wout_ref[0:512, :]) + _dot(ym.astype(BF16), wout_ref[512:1024, :])
    y_ref[0] = x_ref[0] + gate_ref[0, 2:3, :] * o


def _prompt_out(qp, kp, v, gpc, zmg, x, mod_p, wout, tq):
    b, t, d = x.shape
    tile = lambda w: pl.BlockSpec((1, tq, w), lambda i, j: (i, j, 0))
    full = lambda w: pl.BlockSpec((1, t, w), lambda i, j: (i, 0, 0))
    return pl.pallas_call(
        functools.partial(_prompt_out_kernel, tq=tq, nt=t // tq),
        out_shape=jax.ShapeDtypeStruct((b, t, d), F32),
        grid=(b, t // tq),
        in_specs=[
            tile(512), full(512), full(512), tile(512), tile(512), tile(d),
            pl.BlockSpec((1, 3, d), lambda i, j: (i, 0, 0)),
            pl.BlockSpec((d, d), lambda i, j: (0, 0)),
        ],
        out_specs=tile(d),
        scratch_shapes=[pltpu.VMEM((2, tq, t), F32), pltpu.VMEM((2, tq, t), BF16),
                        pltpu.VMEM((tq, N_HEADS * LANES), F32)],
        compiler_params=pltpu.CompilerParams(
            dimension_semantics=("parallel", "arbitrary"), vmem_limit_bytes=VMEM_LIMIT),
        name="prompt_out",
    )(qp, kp, v, gpc, zmg, x, mod_p, wout)


def _sample_in_kernel(x_ref, mod_ref, g_ref, win_ref, poolw_ref, pscale_ref, convw_ref,
                      qng_ref, wuq_ref, s128q_ref, gq_ref, kvg_ref, krg_ref, kng_ref, wukt_ref, tab_ref,
                      pst_ref, cst_ref,
                      gpc_ref, zmg_ref, qabs_ref, qr_ref, lat_ref, krp_ref, psto_ref, csto_ref,
                      *, db, ts, d, past_len):
    shift = mod_ref[:, 0:d]
    scale1 = 1.0 + mod_ref[:, d:2 * d]
    hs = []
    for t in range(ts):
        x = x_ref[:, t * d:(t + 1) * d]
        ms = jnp.mean(x * x, axis=-1, keepdims=True)
        hs.append(((x * lax.rsqrt(ms + EPS) * g_ref[...]) * scale1 + shift).astype(BF16))
    hb = jnp.concatenate(hs, axis=0)
    rows = lambda a, t: a[t * db:(t + 1) * db]

    pp = _dot(hb, win_ref[:, C_POOL:C_POOL + 512])
    zp = pp[:, 256:512]
    np_ = POOL_MAX - 1
    ext = [pst_ref[k] for k in range(np_)] + [rows(pp, t)[:, 0:256] for t in range(ts)]
    for k in range(np_):
        psto_ref[k] = ext[ts + k]

    def doubled(prev, step, lo):
        return {r: prev[r] + prev[r - step] for r in range(lo, np_ + ts) if r in prev and (r - step) in prev}

    a0 = dict(enumerate(ext))
    a1 = doubled(a0, 1, 1)
    a2 = doubled(a1, 2, 3)
    a3 = doubled(a2, 4, 7)
    a4 = doubled(a3, 8, 15)
    lane = lax.broadcasted_iota(jnp.int32, (1, 256), 1)
    ds_ = []
    for t in range(ts):
        r = np_ + t
        inv = [1.0 / float(min(past_len + t + 1, w)) for w in POOL_WINDOWS]
        mean = jnp.where(lane < 64, a1[r] * inv[0],
                         jnp.where(lane < 128, a2[r] * inv[1],
                                   jnp.where(lane < 192, a3[r] * inv[2], a4[r] * inv[3])))
        ds_.append(mean - ext[r])
    dcat = jnp.concatenate(ds_, axis=0)
    gp = _dot(dcat.astype(BF16), poolw_ref[...]) * pscale_ref[...] * _silu(zp)

    pc = _dot(hb, win_ref[:, C_CONV:C_CONV + 1024])
    bg = pc[:, 256:512]
    zc = pc[:, 768:1024]
    vfull = pc[:, 512:768] * pc[:, 0:256]
    nc = CONV_W - 1
    vx = [cst_ref[:, k * 256:(k + 1) * 256] for k in range(nc)] + [rows(vfull, t) for t in range(ts)]
    for k in range(nc):
        csto_ref[:, k * 256:(k + 1) * 256] = vx[ts + k]
    ys = []
    for t in range(ts):
        acc = convw_ref[0:1, :] * vx[t]
        for k in range(1, CONV_W):
            acc = acc + convw_ref[k:k + 1, :] * vx[t + k]
        ys.append(acc)
    gc = bg * jnp.concatenate(ys, axis=0) * _silu(zc)
    gpc_ref[...] = jnp.concatenate([gp, gc], axis=1).astype(BF16)

    zmg_ref[...] = _silu(_dot(hb, win_ref[:, C_ZM:C_ZM + 512])).astype(BF16)

    pm = _dot(hb, win_ref[:, C_MLA:C_MLA + 512])
    qh, ckv, kr = _mla_prep(pm, qng_ref[...], wuq_ref[...], s128q_ref[...], gq_ref[...],
                            kvg_ref[...], krg_ref[...], tab_ref[0], tab_ref[1], tab_ref[2])
    for t in range(ts):
        lat_ref[:, t * LANES:(t + 1) * LANES] = rows(ckv, t)
        krp_ref[:, t * LANES:(t + 1) * LANES] = rows(kr, t)
    for h in range(N_HEADS):
        qa = _dot((qh[h] * kng_ref[...]).astype(BF16), wukt_ref[h])
        for t in range(ts):
            c0 = (h * ts + t) * LANES
            qabs_ref[:, c0:c0 + LANES] = rows(qa, t).astype(BF16)
            qr_ref[:, c0:c0 + LANES] = rows(qh[h], t).astype(BF16)


def _sample_in(xs2, mod_s, lw, tab, pst, cst, ts, past_len):
    db = xs2.shape[0]
    d = xs2.shape[1] // ts
    r = ts * db
    outs = [
        jax.ShapeDtypeStruct((r, 512), BF16),
        jax.ShapeDtypeStruct((r, 512), BF16),
        jax.ShapeDtypeStruct((db, N_HEADS * ts * LANES), BF16),
        jax.ShapeDtypeStruct((db, N_HEADS * ts * LANES), BF16),
        jax.ShapeDtypeStruct((db, ts * LANES), F32),
        jax.ShapeDtypeStruct((db, ts * LANES), F32),
        jax.ShapeDtypeStruct(pst.shape, F32),
        jax.ShapeDtypeStruct(cst.shape, F32),
    ]
    return pl.pallas_call(
        functools.partial(_sample_in_kernel, db=db, ts=ts, d=d, past_len=past_len),
        out_shape=outs,
        compiler_params=pltpu.CompilerParams(vmem_limit_bytes=VMEM_LIMIT),
        name="sample_in",
    )(xs2, mod_s, lw["norm_g"], lw["win"], lw["poolw"], lw["pscale"], lw["convw"],
      lw["qng"], lw["wuq"], lw["s128q"], lw["gq"], lw["kvg"], lw["krg"], lw["kngpad"], lw["wukt_pad"], tab,
      pst, cst)


def _sample_attn_kernel(pt_ref, qabs_ref, qr_ref, latn_ref, krn_ref, wukt_ref, lat_hbm, krt_hbm, out_ref,
                        lhs, newlat, newkr, lat_buf, krt_buf, latb, s_scr, mx_scr, lat_t, sem, m_s, l_s, acc_s,
                        *, layer, g, ts, page, n_pages):
    b = pl.program_id(0)
    nb = pl.num_programs(0)
    nc = n_pages // g
    nq = N_HEADS * ts
    nk = N_HEADS * NOPE
    half = (b % 2) * nc
    other = nc - half

    def page_copies(seq, chunk, slot, j):
        pg = pt_ref[seq * n_pages + chunk * g + j]
        return (pltpu.make_async_copy(lat_hbm.at[layer, pg], lat_buf.at[slot, j], sem.at[0, slot]),
                pltpu.make_async_copy(krt_hbm.at[layer, pg], krt_buf.at[slot, j], sem.at[1, slot]))

    def wait_chunk(slot):
        pltpu.make_async_copy(lat_hbm.at[layer, pl.ds(0, g)], lat_buf.at[slot], sem.at[0, slot]).wait()
        pltpu.make_async_copy(krt_hbm.at[layer, pl.ds(0, g)], krt_buf.at[slot], sem.at[1, slot]).wait()

    def scores(lat_bf, s_rope, lat_t_bf=None):
        res = _dot_nt(lhs[...], lat_bf) if lat_t_bf is None else _dot(lhs[...], lat_t_bf)
        kt = res[0:nk]
        k2 = kt * kt
        rr = []
        for h in range(N_HEADS):
            ss = jnp.sum(k2[h * NOPE:(h + 1) * NOPE], axis=0, keepdims=True)
            rr.append(jnp.broadcast_to(lax.rsqrt(ss * (1.0 / NOPE) + EPS), (ts, lat_bf.shape[0])))
        return res[nk:nk + nq] * jnp.concatenate(rr, axis=0) + s_rope

    def tail(slot):
        m_old = m_s[...]
        m_new = jnp.maximum(m_old, mx_scr[slot])
        corr = jnp.exp(m_old - m_new)
        psum = jnp.zeros((nq, LANES), F32)
        ps = []
        for t in range(g * page // LANES):
            p_t = jnp.exp(s_scr[slot, :, t * LANES:(t + 1) * LANES] - m_new)
            psum = psum + p_t
            ps.append(p_t.astype(BF16))
        l_s[...] = l_s[...] * corr + psum
        acc_s[...] = acc_s[...] * corr + _dot(jnp.concatenate(ps, axis=1), latb[slot])
        m_s[...] = m_new

    @pl.when(b == 0)
    def _():
        for c in range(nc):
            for j in range(g):
                for cp in page_copies(0, c, c, j):
                    cp.start()
        lhs[0:nk, :] = wukt_ref[...]
        newlat[...] = jnp.zeros((page, LANES), F32)
        newkr[...] = jnp.zeros((page, ROPE), F32)
        latb[1] = jnp.zeros((g * page, LANES), BF16)

    for c in range(nc):
        wait_chunk(half + c)

    qr_bf = qr_ref[0].astype(F32)[:, NOPE:NOPE + ROPE].astype(BF16)
    lhs[nk:nk + nq, :] = qabs_ref[0]
    newlat[0:ts, :] = latn_ref[0]
    newkr[0:ts, :] = krn_ref[0][:, NOPE:NOPE + ROPE]
    m_s[...] = jnp.full((nq, LANES), NEG, F32)
    l_s[...] = jnp.zeros((nq, LANES), F32)
    acc_s[...] = jnp.zeros((nq, LANES), F32)
    lat_bf = newlat[...].astype(BF16)
    s = scores(lat_bf, _dot_nt(qr_bf, newkr[...].astype(BF16)))
    key = lax.broadcasted_iota(jnp.int32, (nq, page), 1)
    qi = lax.broadcasted_iota(jnp.int32, (nq, page), 0) % ts
    s = jnp.where(key <= qi, s, NEG)
    s_scr[1] = jnp.full((nq, g * page), NEG, F32)
    s_scr[1, :, 0:page] = s
    mx_scr[1] = jnp.broadcast_to(jnp.max(s, axis=-1, keepdims=True), (nq, LANES))
    latb[1, 0:page, :] = lat_bf

    def head(c):
        par = c % 2
        parts = []
        for jj in range(g // 2):
            for cn in ((2 * c, 2 * c + 1) if 2 * c + 1 < nc else ()):
                for j in (2 * jj, 2 * jj + 1):
                    @pl.when(b + 1 < nb)
                    def _(cn=cn, j=j):
                        for cp in page_copies(b + 1, cn, other + cn, j):
                            cp.start()
            lat_f = lat_buf[half + c, 2 * jj:2 * jj + 2].reshape(2 * page, LANES)
            lat_bf = lat_f.astype(BF16)
            latb[par, 2 * jj * page:(2 * jj + 2) * page, :] = lat_bf
            kr2 = jnp.concatenate([krt_buf[half + c, 2 * jj], krt_buf[half + c, 2 * jj + 1]],
                                  axis=1).astype(BF16)
            lat_t[jj % 2] = lat_bf.T
            parts.append(scores(lat_bf, _dot(qr_bf, kr2), lat_t[jj % 2]))
        s_scr[par] = jnp.concatenate(parts, axis=1)
        mx = parts[0]
        for part in parts[1:]:
            mx = jnp.maximum(mx, part)
        mx_scr[par] = jnp.broadcast_to(jnp.max(mx, axis=-1, keepdims=True), (nq, LANES))

    for c in range(nc):
        tail(1 - c % 2)
        head(c)
    tail((nc - 1) % 2)
    out_ref[0] = acc_s[...] / jnp.sum(l_s[...], axis=-1, keepdims=True)


def _sample_attn(page_table, qabs, qr, latn, krn, wukt, cache_latent, cache_krope_t, layer, g, ts):
    db, n_pages = page_table.shape
    page = cache_latent.shape[2]
    nc = n_pages // g
    assert n_pages % g == 0 and nc % 2 == 0
    nq = N_HEADS * ts
    per_seq = lambda rows, w: pl.BlockSpec((1, rows, w), lambda b, pt: (b, 0, 0))
    grid_spec = pltpu.PrefetchScalarGridSpec(
        num_scalar_prefetch=1,
        grid=(db,),
        in_specs=[per_seq(nq, LANES), per_seq(nq, LANES), per_seq(ts, LANES), per_seq(ts, LANES),
                  pl.BlockSpec((N_HEADS * NOPE, LANES), lambda b, pt: (0, 0)),
                  pl.BlockSpec(memory_space=pl.ANY), pl.BlockSpec(memory_space=pl.ANY)],
        out_specs=per_seq(nq, LANES),
        scratch_shapes=[
            pltpu.VMEM((N_HEADS * NOPE + nq, LANES), BF16),
            pltpu.VMEM((page, LANES), F32), pltpu.VMEM((page, ROPE), F32),
            pltpu.VMEM((2 * nc, g, page, LANES), F32),
            pltpu.VMEM((2 * nc, g, ROPE, page), F32),
            pltpu.VMEM((2, g * page, LANES), BF16),
            pltpu.VMEM((2, nq, g * page), F32),
            pltpu.VMEM((2, nq, LANES), F32),
            pltpu.VMEM((2, LANES, 2 * page), BF16),
            pltpu.SemaphoreType.DMA((2, 2 * nc)),
            pltpu.VMEM((nq, LANES), F32), pltpu.VMEM((nq, LANES), F32), pltpu.VMEM((nq, LANES), F32),
        ],
    )
    return pl.pallas_call(
        functools.partial(_sample_attn_kernel, layer=layer, g=g, ts=ts, page=page, n_pages=n_pages),
        out_shape=jax.ShapeDtypeStruct((db, nq, LANES), F32),
        grid_spec=grid_spec,
        compiler_params=pltpu.CompilerParams(
            dimension_semantics=("arbitrary",), vmem_limit_bytes=VMEM_LIMIT),
        name="sample_attn",
    )(page_table.reshape(-1), qabs, qr, latn, krn, wukt, cache_latent, cache_krope_t)


def _sample_out_kernel(lo_ref, wuv_ref, gpc_ref, zmg_ref, x_ref, mod_ref, wout_ref, y_ref, *, db, ts, d):
    gate = mod_ref[:, 2 * d:3 * d]
    for t in range(ts):
        ym = jnp.concatenate(
            [_dot(lo_ref[:, (h * ts + t) * LANES:(h * ts + t + 1) * LANES].astype(BF16),
                  wuv_ref[:, h * LANES:(h + 1) * LANES]) for h in range(N_HEADS)], axis=1)
        ymg = (ym * zmg_ref[t * db:(t + 1) * db, :].astype(F32)).astype(BF16)
        o = _dot(gpc_ref[t * db:(t + 1) * db, :], wout_ref[0:512, :]) + _dot(ymg, wout_ref[512:1024, :])
        y_ref[:, t * d:(t + 1) * d] = x_ref[:, t * d:(t + 1) * d] + gate * o


def _sample_out(lat_out2, wuv, gpc, zmg, xs2, mod_s, wout, ts):
    db = xs2.shape[0]
    d = xs2.shape[1] // ts
    return pl.pallas_call(
        functools.partial(_sample_out_kernel, db=db, ts=ts, d=d),
        out_shape=jax.ShapeDtypeStruct(xs2.shape, F32),
        compiler_params=pltpu.CompilerParams(vmem_limit_bytes=VMEM_LIMIT),
        name="sample_out",
    )(lat_out2, wuv, gpc, zmg, xs2, mod_s, wout)


def _rope_tables(pos):
    inv_freq = 1.0 / (ROPE_THETA ** (jnp.arange(0, ROPE, 2, dtype=F32) / ROPE))
    ang = pos.astype(F32)[:, None] * inv_freq[None, :]
    cos, sin = jnp.cos(ang), jnp.sin(ang)
    n = pos.shape[0]
    half = ROPE // 2
    z = lambda w: jnp.zeros((n, w), F32)
    cos_t = jnp.concatenate([jnp.ones((n, NOPE), F32), cos, cos, jnp.ones((n, LANES - NOPE - ROPE), F32)], axis=1)
    sin_a = jnp.concatenate([z(NOPE), -sin, z(LANES - NOPE - half)], axis=1)
    sin_b = jnp.concatenate([z(NOPE + half), sin, z(LANES - NOPE - ROPE)], axis=1)
    return jnp.stack([cos_t, sin_a, sin_b])


def _all_layer_weights(norm_g, w_in, pool_w, pool_scale, conv_w, q_norm_g, w_uq, qn_g, qr_g,
                       kv_norm_g, kr_g, w_uk, kn_g, w_uv, w_out):
    depth, d, _ = w_in.shape
    wt = jnp.swapaxes(w_in, 1, 2)
    zr = lambda n: jnp.zeros((depth, n, d), F32)
    wt = jnp.concatenate([wt[:, 0:1920], zr(NOPE), wt[:, 1920:1952], zr(LANES - NOPE - ROPE), wt[:, 1952:]], axis=1)
    win = jnp.swapaxes(wt, 1, 2).astype(BF16)
    pad_head = lambda w, per: jnp.concatenate(
        [jnp.concatenate([w[:, :, h * per:(h + 1) * per],
                          jnp.zeros((depth, w.shape[1], LANES - per), F32)], axis=2)
         for h in range(N_HEADS)], axis=2)
    seg = jnp.arange(LANES)
    nope_m = (seg < NOPE).astype(F32)
    rope_m = ((seg >= NOPE) & (seg < NOPE + ROPE)).astype(F32)
    s128k = nope_m[:, None] * nope_m[None, :] / NOPE
    s128q = s128k + rope_m[:, None] * rope_m[None, :] / ROPE
    per_layer = lambda a: jnp.broadcast_to(a, (depth,) + a.shape)
    zl = lambda w: jnp.zeros((depth, w), F32)
    gw = pool_w.shape[2]
    poolw = jnp.zeros((depth, 256, 256), F32)
    for gi in range(pool_w.shape[1]):
        poolw = poolw.at[:, gi * gw:(gi + 1) * gw, gi * gw:(gi + 1) * gw].set(pool_w[:, gi])
    wukt = jnp.swapaxes(w_uk, 1, 2)
    wukt_pad = jnp.stack([jnp.concatenate([wukt[:, h * NOPE:(h + 1) * NOPE],
                                           jnp.zeros((depth, LANES - NOPE, w_uk.shape[1]), F32)], axis=1)
                          for h in range(N_HEADS)], axis=1)
    kngpad = jnp.concatenate([kn_g, zl(LANES - NOPE)], axis=1)[:, None, :]
    return {
        "norm_g": norm_g[:, None, :],
        "win": win,
        "poolw": poolw.astype(BF16),
        "pscale": pool_scale[:, None, :],
        "convw": conv_w,
        "qng": q_norm_g[:, None, :],
        "wuq": pad_head(w_uq, NOPE + ROPE).astype(BF16),
        "s128q": per_layer(s128q.astype(BF16)),
        "s128k": per_layer(s128k.astype(BF16)),
        "gq": (jnp.concatenate([qn_g, qr_g, zl(LANES - NOPE - ROPE)], axis=1) * SM_SCALE)[:, None, :],
        "kvg": kv_norm_g[:, None, :],
        "krg": jnp.concatenate([zl(NOPE), kr_g, zl(LANES - NOPE - ROPE)], axis=1)[:, None, :],
        "wuk": pad_head(w_uk, NOPE).astype(BF16),
        "gk": kngpad,
        "kngpad": kngpad,
        "wuv": w_uv.astype(BF16),
        "wukt": wukt.astype(BF16),
        "wukt_pad": wukt_pad.astype(BF16),
        "wout": w_out.astype(BF16),
    }


def kernel(x_prompt, x_sample, cache_latent, cache_krope, state_pool, state_conv, page_table,
           c_prompt, c_sample, norm_g, w_ada, b_ada, w_in, pool_w, pool_scale, conv_w,
           q_norm_g, w_uq, qn_g, qr_g, kv_norm_g, kr_g, w_uk, kn_g, w_uv, w_out):
    depth = w_in.shape[0]
    bp, seq, d = x_prompt.shape
    db, ts, _ = x_sample.shape
    page = cache_latent.shape[2]
    past_len = page_table.shape[1] * page
    ts_p = 512 if seq % 512 == 0 else seq
    tq = 512 if seq % 512 == 0 else seq
    g_pages = 64
    cache_krope_t = jnp.swapaxes(cache_krope, 2, 3)

    mod_prompt, mod_sample = _modulation(c_prompt, c_sample, w_ada, b_ada)
    tab_p = _rope_tables(jnp.arange(seq, dtype=jnp.int32))
    tab_s = jnp.repeat(_rope_tables(past_len + jnp.arange(ts, dtype=jnp.int32)), db, axis=1)
    stacked = _all_layer_weights(norm_g, w_in, pool_w, pool_scale, conv_w, q_norm_g, w_uq, qn_g, qr_g,
                                 kv_norm_g, kr_g, w_uk, kn_g, w_uv, w_out)
    lws = [{name: arr[l] for name, arr in stacked.items()} for l in range(depth)]

    xp = x_prompt
    lat_p, kr_p, pool_p, conv_p = [], [], [], []
    for l in range(depth):
        mod_p = mod_prompt[l].reshape(bp, 3, d)
        gpc, zmg, qp, kp, v, lat, kr, pst, cst = _prompt_in(xp, mod_p, lws[l], tab_p, ts_p)
        xp = _prompt_out(qp, kp, v, gpc, zmg, xp, mod_p, lws[l]["wout"], tq)
        lat_p.append(lat)
        kr_p.append(kr)
        pool_p.append(pst)
        conv_p.append(cst)

    xs2 = x_sample.reshape(db, ts * d)
    lat_s, kr_s, pool_s, conv_s = [], [], [], []
    for l in range(depth):
        mod_s = mod_sample[l]
        pst = jnp.swapaxes(state_pool[l], 0, 1)
        cst = state_conv[l].reshape(db, -1)
        gpc, zmg, qabs, qr, latn, krn, psto, csto = _sample_in(xs2, mod_s, lws[l], tab_s, pst, cst, ts, past_len)
        nq = N_HEADS * ts
        lat_out = _sample_attn(page_table, qabs.reshape(db, nq, LANES), qr.reshape(db, nq, LANES),
                               latn.reshape(db, ts, LANES), krn.reshape(db, ts, LANES), lws[l]["wukt"],
                               cache_latent, cache_krope_t, l, g_pages, ts)
        xs2 = _sample_out(lat_out.reshape(db, nq * LANES), lws[l]["wuv"], gpc, zmg, xs2, mod_s,
                          lws[l]["wout"], ts)
        lat_s.append(latn.reshape(db, ts, LANES))
        kr_s.append(krn.reshape(db, ts, LANES)[:, :, NOPE:NOPE + ROPE])
        pool_s.append(jnp.swapaxes(psto, 0, 1))
        conv_s.append(csto.reshape(state_conv.shape[1:]))

    return (xp, xs2.reshape(x_sample.shape),
            jnp.stack(lat_p), jnp.stack(kr_p), jnp.stack(pool_p), jnp.stack(conv_p),
            jnp.stack(lat_s), jnp.stack(kr_s), jnp.stack(pool_s), jnp.stack(conv_s))
```

```python
import functools
import math

import jax
import jax.numpy as jnp
from jax import lax
from jax.experimental import pallas as pl
from jax.experimental.pallas import tpu as pltpu

F32 = jnp.float32
BF16 = jnp.bfloat16

POOL_WINDOWS = (2, 4, 8, 16)
POOL_MAX = max(POOL_WINDOWS)
CONV_W = 3
N_HEADS = 4
NOPE = 64
ROPE = 32
ROPE_THETA = 10000.0
EPS = 1e-6
NEG = -1e30
SM_SCALE = 1.0 / math.sqrt(NOPE + ROPE)
LANES = 128
HALO = 16
VMEM_LIMIT = 56 * 1024 * 1024

C_POOL = 0
C_CONV = 512
C_MLA = 1536
C_ZM = 2048
D_IN_PAD = 2560


def _sigmoid(x):
    return 1.0 / (1.0 + jnp.exp(-x))


def _silu(x):
    return x * _sigmoid(x)


def _dot(a, b):
    return jnp.dot(a, b, preferred_element_type=F32)


def _dot_nt(a, b):
    return lax.dot_general(a, b, (((1,), (1,)), ((), ())), preferred_element_type=F32)


def _rms(x, n):
    return x * lax.rsqrt(jnp.sum(x * x, axis=-1, keepdims=True) * (1.0 / n) + EPS)


def _rope_block(x, cos_t, sin_a, sin_b):
    return x * cos_t + pltpu.roll(x, LANES - ROPE // 2, 1) * sin_a + pltpu.roll(x, ROPE // 2, 1) * sin_b


def _mod_kernel(cp_ref, cs_ref, w_ref, b_ref, op_ref, os_ref):
    w = w_ref[0].astype(BF16)
    op_ref[0] = _dot(_silu(cp_ref[...]).astype(BF16), w) + b_ref[0]
    os_ref[0] = _dot(_silu(cs_ref[...]).astype(BF16), w) + b_ref[0]


def _modulation(c_prompt, c_sample, w_ada, b_ada):
    depth, d, d3 = w_ada.shape
    bp, db = c_prompt.shape[0], c_sample.shape[0]
    nb = d3 // d
    return pl.pallas_call(
        _mod_kernel,
        out_shape=[jax.ShapeDtypeStruct((depth, bp, d3), F32), jax.ShapeDtypeStruct((depth, db, d3), F32)],
        grid=(depth, nb),
        in_specs=[
            pl.BlockSpec((bp, d), lambda l, j: (0, 0)),
            pl.BlockSpec((db, d), lambda l, j: (0, 0)),
            pl.BlockSpec((1, d, d), lambda l, j: (l, 0, j)),
            pl.BlockSpec((1, 1, d), lambda l, j: (l, 0, j)),
        ],
        out_specs=[pl.BlockSpec((1, bp, d), lambda l, j: (l, 0, j)),
                   pl.BlockSpec((1, db, d), lambda l, j: (l, 0, j))],
        compiler_params=pltpu.CompilerParams(vmem_limit_bytes=VMEM_LIMIT),
        name="adaln_mod",
    )(c_prompt, c_sample, w_ada, b_ada.reshape(depth, 1, d3))


def _mla_prep(pm, qng, wuq, s128q, gq, kvg, krg, cos_t, sin_a, sin_b):
    cq = pm[:, 0:256]
    ckv_raw = pm[:, 256:384]
    kr_raw = pm[:, 384:512]
    cqn = (_rms(cq, 256) * qng).astype(BF16)
    q = _dot(cqn, wuq)
    qh = []
    for h in range(N_HEADS):
        xh = q[:, h * LANES:(h + 1) * LANES]
        ms = _dot((xh * xh).astype(BF16), s128q)
        xh = xh * lax.rsqrt(ms + EPS) * gq
        qh.append(_rope_block(xh, cos_t, sin_a, sin_b))
    ckv = _rms(ckv_raw, 128) * kvg
    krn = _rms(kr_raw, ROPE) * krg
    kr = _rope_block(krn, cos_t, sin_a, sin_b)
    return qh, ckv, kr


def _prompt_in_kernel(x_ref, mod_ref, g_ref, win_ref, poolw_ref, pscale_ref, convw_ref,
                      qng_ref, wuq_ref, s128q_ref, gq_ref, kvg_ref, krg_ref,
                      wuk_ref, s128k_ref, gk_ref, wuv_ref, tab_ref,
                      gpc_ref, zmg_ref, qp_ref, kp_ref, v_ref, lat_ref, kr_ref, pst_ref, cst_ref,
                      uext, vext, *, ts, nt):
    t = pl.program_id(1)
    x = x_ref[0]
    ms = jnp.mean(x * x, axis=-1, keepdims=True)
    h = (x * lax.rsqrt(ms + EPS) * g_ref[...]) * (1.0 + mod_ref[0, 1:2, :]) + mod_ref[0, 0:1, :]
    hb = h.astype(BF16)

    @pl.when(t == 0)
    def _():
        uext[0:HALO, :] = jnp.zeros((HALO, 256), F32)
        vext[0:HALO, :] = jnp.zeros((HALO, 256), F32)

    pp = _dot(hb, win_ref[:, C_POOL:C_POOL + 512])
    u = pp[:, 0:256]
    zp = pp[:, 256:512]
    uext[HALO:HALO + ts, :] = u
    ext = uext[...]
    a1 = ext + pltpu.roll(ext, 1, 0)
    a2 = a1 + pltpu.roll(a1, 2, 0)
    a3 = a2[:, 128:256]
    a3 = a3 + pltpu.roll(a3, 4, 0)
    a4 = a3 + pltpu.roll(a3, 8, 0)
    pos1 = t * ts + lax.broadcasted_iota(jnp.int32, (ts, 1), 0) + 1
    inv = [1.0 / jnp.minimum(pos1, w).astype(F32) for w in POOL_WINDOWS]
    lane = lax.broadcasted_iota(jnp.int32, (1, LANES), 1)
    mean_l = jnp.where(lane < 64, a1[HALO:, 0:128] * inv[0], a2[HALO:, 0:128] * inv[1])
    mean_r = jnp.where(lane < 64, a3[HALO:] * inv[2], a4[HALO:] * inv[3])
    d = jnp.concatenate([mean_l, mean_r], axis=1) - u
    yp = _dot(d.astype(BF16), poolw_ref[...]) * pscale_ref[...]
    gp = yp * _silu(zp)
    uext[0:HALO, :] = u[ts - HALO:ts, :]

    pc = _dot(hb, win_ref[:, C_CONV:C_CONV + 1024])
    hc = pc[:, 0:256]
    bg = pc[:, 256:512]
    cg = pc[:, 512:768]
    zc = pc[:, 768:1024]
    v = cg * hc
    vext[HALO:HALO + ts, :] = v
    e = vext[...]
    y = convw_ref[0:1, :] * pltpu.roll(e, 2, 0) + convw_ref[1:2, :] * pltpu.roll(e, 1, 0) + convw_ref[2:3, :] * e
    gc = bg * y[HALO:] * _silu(zc)
    gpc_ref[0] = jnp.concatenate([gp, gc], axis=1).astype(BF16)
    vext[0:HALO, :] = v[ts - HALO:ts, :]

    zmg_ref[0] = _silu(_dot(hb, win_ref[:, C_ZM:C_ZM + 512])).astype(BF16)

    pm = _dot(hb, win_ref[:, C_MLA:C_MLA + 512])
    cos_t = tab_ref[0]
    sin_a = tab_ref[1]
    sin_b = tab_ref[2]
    qh, ckv, kr = _mla_prep(pm, qng_ref[...], wuq_ref[...], s128q_ref[...], gq_ref[...],
                            kvg_ref[...], krg_ref[...], cos_t, sin_a, sin_b)
    qp_ref[0] = jnp.concatenate(qh, axis=1).astype(BF16)
    lat_ref[0] = ckv
    kr_ref[0] = kr[:, NOPE:NOPE + ROPE]
    cb = ckv.astype(BF16)
    kc = _dot(cb, wuk_ref[...])
    kh = []
    for hh in range(N_HEADS):
        xh = kc[:, hh * LANES:(hh + 1) * LANES]
        msk = _dot((xh * xh).astype(BF16), s128k_ref[...])
        kh.append(xh * lax.rsqrt(msk + EPS) * gk_ref[...] + kr)
    kp_ref[0] = jnp.concatenate(kh, axis=1).astype(BF16)
    v_ref[0] = _dot(cb, wuv_ref[...]).astype(BF16)

    @pl.when(t == nt - 1)
    def _():
        pst_ref[0] = uext[pl.ds(HALO + ts - (POOL_MAX - 1), POOL_MAX - 1), :]
        cst_ref[0] = vext[pl.ds(HALO + ts - (CONV_W - 1), CONV_W - 1), :]


def _prompt_in(x, mod_p, lw, tab, ts):
    b, t, d = x.shape
    nt = t // ts
    const = lambda shape: pl.BlockSpec(shape, lambda i, j: (0,) * len(shape))
    tile = lambda w: pl.BlockSpec((1, ts, w), lambda i, j: (i, j, 0))
    outs = [
        jax.ShapeDtypeStruct((b, t, 512), BF16),
        jax.ShapeDtypeStruct((b, t, 512), BF16),
        jax.ShapeDtypeStruct((b, t, 512), BF16),
        jax.ShapeDtypeStruct((b, t, 512), BF16),
        jax.ShapeDtypeStruct((b, t, 512), BF16),
        jax.ShapeDtypeStruct((b, t, 128), F32),
        jax.ShapeDtypeStruct((b, t, ROPE), F32),
        jax.ShapeDtypeStruct((b, POOL_MAX - 1, 256), F32),
        jax.ShapeDtypeStruct((b, CONV_W - 1, 256), F32),
    ]
    return pl.pallas_call(
        functools.partial(_prompt_in_kernel, ts=ts, nt=nt),
        out_shape=outs,
        grid=(b, nt),
        in_specs=[
            tile(d),
            pl.BlockSpec((1, 3, d), lambda i, j: (i, 0, 0)),
            const((1, d)),
            const((d, D_IN_PAD)),
            const((256, 256)), const((1, 256)), const((CONV_W, 256)),
            const((1, 256)), const((256, 512)), const((128, 128)), const((1, 128)),
            const((1, 128)), const((1, 128)),
            const((128, 512)), const((128, 128)), const((1, 128)), const((128, 512)),
            pl.BlockSpec((3, ts, 128), lambda i, j: (0, j, 0)),
        ],
        out_specs=[
            tile(512), tile(512), tile(512), tile(512), tile(512), tile(128), tile(ROPE),
            pl.BlockSpec((1, POOL_MAX - 1, 256), lambda i, j: (i, 0, 0)),
            pl.BlockSpec((1, CONV_W - 1, 256), lambda i, j: (i, 0, 0)),
        ],
        scratch_shapes=[pltpu.VMEM((HALO + ts, 256), F32), pltpu.VMEM((HALO + ts, 256), F32)],
        compiler_params=pltpu.CompilerParams(
            dimension_semantics=("parallel", "arbitrary"), vmem_limit_bytes=VMEM_LIMIT),
        name="prompt_in",
    )(x, mod_p, lw["norm_g"], lw["win"], lw["poolw"], lw["pscale"], lw["convw"],
      lw["qng"], lw["wuq"], lw["s128q"], lw["gq"], lw["kvg"], lw["krg"],
      lw["wuk"], lw["s128k"], lw["gk"], lw["wuv"], tab)


def _prompt_out_kernel(q_ref, k_ref, v_ref, gpc_ref, zmg_ref, x_ref, gate_ref, wout_ref, y_ref,
                       s_scr, p_scr, ym_scr, *, tq, nt):
    i = pl.program_id(1)
    row = lax.broadcasted_iota(jnp.int32, (tq, LANES), 0)
    lane = lax.broadcasted_iota(jnp.int32, (tq, LANES), 1)
    n_diag = tq // LANES

    def attend(n_keys):
        n_pieces = n_keys // LANES
        for h in range(N_HEADS):
            par = h % 2
            sl = slice(h * LANES, (h + 1) * LANES)
            s_scr[par, :, 0:n_keys] = _dot_nt(q_ref[0, :, sl], k_ref[0, 0:n_keys, sl])

            def piece(t):
                x = s_scr[par, :, t * LANES:(t + 1) * LANES]
                td = t - (n_pieces - n_diag)
                if td >= 0:
                    x = jnp.where(td * LANES + lane <= row, x, NEG)
                return x

            mx = piece(0)
            for t in range(1, n_pieces):
                mx = jnp.maximum(mx, piece(t))
            m_rep = jnp.broadcast_to(jnp.max(mx, axis=-1, keepdims=True), (tq, LANES))
            psum = jnp.zeros((tq, LANES), F32)
            for t in range(n_pieces):
                p_t = jnp.exp(piece(t) - m_rep)
                psum = psum + p_t
                p_scr[par, :, t * LANES:(t + 1) * LANES] = p_t.astype(BF16)
            o = _dot(p_scr[par, :, 0:n_keys], v_ref[0, 0:n_keys, sl])
            ym_scr[:, sl] = o / jnp.sum(psum, axis=-1, keepdims=True)

    for v in range(nt):
        pl.when(i == v)(functools.partial(attend, (v + 1) * tq))

    ym = ym_scr[...] * zmg_ref[0].astype(F32)
    o = _dot(gpc_ref[0], wout_ref[0:512, :]) + _dot(ym.astype(BF16), wout_ref[512:1024, :])
    y_ref[0] = x_ref[0] + gate_ref[0, 2:3, :] * o


def _prompt_out(qp, kp, v, gpc, zmg, x, mod_p, wout, tq):
    b, t, d = x.shape
    tile = lambda w: pl.BlockSpec((1, tq, w), lambda i, j: (i, j, 0))
    full = lambda w: pl.BlockSpec((1, t, w), lambda i, j: (i, 0, 0))
    return pl.pallas_call(
        functools.partial(_prompt_out_kernel, tq=tq, nt=t // tq),
        out_shape=jax.ShapeDtypeStruct((b, t, d), F32),
        grid=(b, t // tq),
        in_specs=[
            tile(512), full(512), full(512), tile(512), tile(512), tile(d),
            pl.BlockSpec((1, 3, d), lambda i, j: (i, 0, 0)),
            pl.BlockSpec((d, d), lambda i, j: (0, 0)),
        ],
        out_specs=tile(d),
        scratch_shapes=[pltpu.VMEM((2, tq, t), F32), pltpu.VMEM((2, tq, t), BF16),
                        pltpu.VMEM((tq, N_HEADS * LANES), F32)],
        compiler_params=pltpu.CompilerParams(
            dimension_semantics=("parallel", "arbitrary"), vmem_limit_bytes=VMEM_LIMIT),
        name="prompt_out",
    )(qp, kp, v, gpc, zmg, x, mod_p, wout)


def _sample_in_kernel(x_ref, mod_ref, g_ref, win_ref, poolw_ref, pscale_ref, convw_ref,
                      qng_ref, wuq_ref, s128q_ref, gq_ref, kvg_ref, krg_ref, kng_ref, wukt_ref, tab_ref,
                      pst_ref, cst_ref,
                      gpc_ref, zmg_ref, qabs_ref, qr_ref, lat_ref, krp_ref, psto_ref, csto_ref,
                      *, db, ts, d, past_len):
    shift = mod_ref[:, 0:d]
    scale1 = 1.0 + mod_ref[:, d:2 * d]
    hs = []
    for t in range(ts):
        x = x_ref[:, t * d:(t + 1) * d]
        ms = jnp.mean(x * x, axis=-1, keepdims=True)
        hs.append(((x * lax.rsqrt(ms + EPS) * g_ref[...]) * scale1 + shift).astype(BF16))
    hb = jnp.concatenate(hs, axis=0)
    rows = lambda a, t: a[t * db:(t + 1) * db]

    pp = _dot(hb, win_ref[:, C_POOL:C_POOL + 512])
    zp = pp[:, 256:512]
    np_ = POOL_MAX - 1
    ext = [pst_ref[k] for k in range(np_)] + [rows(pp, t)[:, 0:256] for t in range(ts)]
    for k in range(np_):
        psto_ref[k] = ext[ts + k]

    def doubled(prev, step, lo):
        return {r: prev[r] + prev[r - step] for r in range(lo, np_ + ts) if r in prev and (r - step) in prev}

    a0 = dict(enumerate(ext))
    a1 = doubled(a0, 1, 1)
    a2 = doubled(a1, 2, 3)
    a3 = doubled(a2, 4, 7)
    a4 = doubled(a3, 8, 15)
    lane = lax.broadcasted_iota(jnp.int32, (1, 256), 1)
    ds_ = []
    for t in range(ts):
        r = np_ + t
        inv = [1.0 / float(min(past_len + t + 1, w)) for w in POOL_WINDOWS]
        mean = jnp.where(lane < 64, a1[r] * inv[0],
                         jnp.where(lane < 128, a2[r] * inv[1],
                                   jnp.where(lane < 192, a3[r] * inv[2], a4[r] * inv[3])))
        ds_.append(mean - ext[r])
    dcat = jnp.concatenate(ds_, axis=0)
    gp = _dot(dcat.astype(BF16), poolw_ref[...]) * pscale_ref[...] * _silu(zp)

    pc = _dot(hb, win_ref[:, C_CONV:C_CONV + 1024])
    bg = pc[:, 256:512]
    zc = pc[:, 768:1024]
    vfull = pc[:, 512:768] * pc[:, 0:256]
    nc = CONV_W - 1
    vx = [cst_ref[:, k * 256:(k + 1) * 256] for k in range(nc)] + [rows(vfull, t) for t in range(ts)]
    for k in range(nc):
        csto_ref[:, k * 256:(k + 1) * 256] = vx[ts + k]
    ys = []
    for t in range(ts):
        acc = convw_ref[0:1, :] * vx[t]
        for k in range(1, CONV_W):
            acc = acc + convw_ref[k:k + 1, :] * vx[t + k]
        ys.append(acc)
    gc = bg * jnp.concatenate(ys, axis=0) * _silu(zc)
    gpc_ref[...] = jnp.concatenate([gp, gc], axis=1).astype(BF16)

    zmg_ref[...] = _silu(_dot(hb, win_ref[:, C_ZM:C_ZM + 512])).astype(BF16)

    pm = _dot(hb, win_ref[:, C_MLA:C_MLA + 512])
    qh, ckv, kr = _mla_prep(pm, qng_ref[...], wuq_ref[...], s128q_ref[...], gq_ref[...],
                            kvg_ref[...], krg_ref[...], tab_ref[0], tab_ref[1], tab_ref[2])
    for t in range(ts):
        lat_ref[:, t * LANES:(t + 1) * LANES] = rows(ckv, t)
        krp_ref[:, t * LANES:(t + 1) * LANES] = rows(kr, t)
    for h in range(N_HEADS):
        qa = _dot((qh[h] * kng_ref[...]).astype(BF16), wukt_ref[h])
        for t in range(ts):
            c0 = (h * ts + t) * LANES
            qabs_ref[:, c0:c0 + LANES] = rows(qa, t).astype(BF16)
            qr_ref[:, c0:c0 + LANES] = rows(qh[h], t).astype(BF16)


def _sample_in(xs2, mod_s, lw, tab, pst, cst, ts, past_len):
    db = xs2.shape[0]
    d = xs2.shape[1] // ts
    r = ts * db
    outs = [
        jax.ShapeDtypeStruct((r, 512), BF16),
        jax.ShapeDtypeStruct((r, 512), BF16),
        jax.ShapeDtypeStruct((db, N_HEADS * ts * LANES), BF16),
        jax.ShapeDtypeStruct((db, N_HEADS * ts * LANES), BF16),
        jax.ShapeDtypeStruct((db, ts * LANES), F32),
        jax.ShapeDtypeStruct((db, ts * LANES), F32),
        jax.ShapeDtypeStruct(pst.shape, F32),
        jax.ShapeDtypeStruct(cst.shape, F32),
    ]
    return pl.pallas_call(
        functools.partial(_sample_in_kernel, db=db, ts=ts, d=d, past_len=past_len),
        out_shape=outs,
        compiler_params=pltpu.CompilerParams(vmem_limit_bytes=VMEM_LIMIT),
        name="sample_in",
    )(xs2, mod_s, lw["norm_g"], lw["win"], lw["poolw"], lw["pscale"], lw["convw"],
      lw["qng"], lw["wuq"], lw["s128q"], lw["gq"], lw["kvg"], lw["krg"], lw["kngpad"], lw["wukt_pad"], tab,
      pst, cst)


def _sample_attn_kernel(pt_ref, qabs_ref, qr_ref, latn_ref, krn_ref, wukt_ref, lat_hbm, krt_hbm, out_ref,
                        lhs, newlat, newkr, lat_buf, krt_buf, latb, s_scr, mx_scr, lat_t, sem, m_s, l_s, acc_s,
                        *, layer, g, ts, page, n_pages):
    b = pl.program_id(0)
    nb = pl.num_programs(0)
    nc = n_pages // g
    nq = N_HEADS * ts
    nk = N_HEADS * NOPE
    PM = lat_t.shape[2] // page
    half = (b % 2) * nc
    other = nc - half

    def page_copies(seq, chunk, slot, j):
        pg = pt_ref[seq * n_pages + chunk * g + j]
        return (pltpu.make_async_copy(lat_hbm.at[layer, pg], lat_buf.at[slot, j], sem.at[0, slot]),
                pltpu.make_async_copy(krt_hbm.at[layer, pg], krt_buf.at[slot, j], sem.at[1, slot]))

    def wait_chunk(slot):
        pltpu.make_async_copy(lat_hbm.at[layer, pl.ds(0, g)], lat_buf.at[slot], sem.at[0, slot]).wait()
        pltpu.make_async_copy(krt_hbm.at[layer, pl.ds(0, g)], krt_buf.at[slot], sem.at[1, slot]).wait()

    def scores(lat_bf, s_rope, lat_t_bf=None):
        res = _dot_nt(lhs[...], lat_bf) if lat_t_bf is None else _dot(lhs[...], lat_t_bf)
        kt = res[0:nk]
        k2 = kt * kt
        rr = []
        for h in range(N_HEADS):
            ss = jnp.sum(k2[h * NOPE:(h + 1) * NOPE], axis=0, keepdims=True)
            rr.append(jnp.broadcast_to(lax.rsqrt(ss * (1.0 / NOPE) + EPS), (ts, lat_bf.shape[0])))
        return res[nk:nk + nq] * jnp.concatenate(rr, axis=0) + s_rope

    def update(n_tiles, tile, s_max, lat_bf_all):
        m_old = m_s[...]
        m_new = jnp.maximum(m_old, s_max)
        corr = jnp.exp(m_old - m_new)
        psum = jnp.zeros((nq, LANES), F32)
        ps = []
        for t in range(n_tiles):
            p_t = jnp.exp(tile(t) - m_new)
            psum = psum + p_t
            ps.append(p_t.astype(BF16))
        l_s[...] = l_s[...] * corr + psum
        p = ps[0] if n_tiles == 1 else jnp.concatenate(ps, axis=1)
        acc_s[...] = acc_s[...] * corr + _dot(p, lat_bf_all)
        m_s[...] = m_new

    @pl.when(b == 0)
    def _():
        for c in range(nc):
            for j in range(g):
                for cp in page_copies(0, c, c, j):
                    cp.start()
        lhs[0:nk, :] = wukt_ref[...]
        newlat[...] = jnp.zeros((page, LANES), F32)
        newkr[...] = jnp.zeros((page, ROPE), F32)

    for c in range(nc):
        wait_chunk(half + c)

    qr_bf = qr_ref[0].astype(F32)[:, NOPE:NOPE + ROPE].astype(BF16)
    lhs[nk:nk + nq, :] = qabs_ref[0]
    newlat[0:ts, :] = latn_ref[0]
    newkr[0:ts, :] = krn_ref[0][:, NOPE:NOPE + ROPE]
    m_s[...] = jnp.full((nq, LANES), NEG, F32)
    l_s[...] = jnp.zeros((nq, LANES), F32)
    acc_s[...] = jnp.zeros((nq, LANES), F32)
    lat_bf = newlat[...].astype(BF16)
    s = scores(lat_bf, _dot_nt(qr_bf, newkr[...].astype(BF16)))
    key = lax.broadcasted_iota(jnp.int32, (nq, page), 1)
    qi = lax.broadcasted_iota(jnp.int32, (nq, page), 0) % ts
    s = jnp.where(key <= qi, s, NEG)
    update(1, lambda t: s, jnp.broadcast_to(jnp.max(s, axis=-1, keepdims=True), (nq, LANES)), lat_bf)

    groups = [(0, (nc - 1) * g), ((nc - 1) * g, n_pages)] if nc > 1 else [(0, n_pages)]

    def head(gi):
        par = gi % 2
        lo, hi = groups[gi]
        parts = []
        for jj, p0 in enumerate(range(lo, hi, PM)):
            c, j0 = p0 // g, p0 % g
            for pn in range(2 * p0, min(2 * p0 + 2 * PM, n_pages)):
                @pl.when(b + 1 < nb)
                def _(cn=pn // g, j=pn % g):
                    for cp in page_copies(b + 1, cn, other + cn, j):
                        cp.start()
            lat_f = lat_buf[half + c, j0:j0 + PM].reshape(PM * page, LANES)
            lat_bf = lat_f.astype(BF16)
            latb[par, PM * jj * page:(PM * jj + PM) * page, :] = lat_bf
            kr2 = jnp.concatenate([krt_buf[half + c, j0 + j] for j in range(PM)], axis=1).astype(BF16)
            lat_t[jj % 2] = lat_bf.T
            parts.append(scores(lat_bf, _dot(qr_bf, kr2), lat_t[jj % 2]))
        s_scr[par, :, 0:(hi - lo) * page] = jnp.concatenate(parts, axis=1)
        mx = parts[0]
        for part in parts[1:]:
            mx = jnp.maximum(mx, part)
        mx_scr[par] = jnp.broadcast_to(jnp.max(mx, axis=-1, keepdims=True), (nq, LANES))

    def tail(gi):
        par = gi % 2
        n_keys = (groups[gi][1] - groups[gi][0]) * page
        update(n_keys // LANES, lambda t: s_scr[par, :, t * LANES:(t + 1) * LANES], mx_scr[par],
               latb[par, 0:n_keys, :])

    for gi in range(len(groups)):
        if gi > 0:
            tail(gi - 1)
        head(gi)
    tail(len(groups) - 1)
    out_ref[0] = acc_s[...] / jnp.sum(l_s[...], axis=-1, keepdims=True)


def _sample_attn(page_table, qabs, qr, latn, krn, wukt, cache_latent, cache_krope_t, layer, g, ts):
    db, n_pages = page_table.shape
    page = cache_latent.shape[2]
    nc = n_pages // g
    assert n_pages % g == 0
    gmax = max(n_pages - g, g)
    nq = N_HEADS * ts
    per_seq = lambda rows, w: pl.BlockSpec((1, rows, w), lambda b, pt: (b, 0, 0))
    grid_spec = pltpu.PrefetchScalarGridSpec(
        num_scalar_prefetch=1,
        grid=(db,),
        in_specs=[per_seq(nq, LANES), per_seq(nq, LANES), per_seq(ts, LANES), per_seq(ts, LANES),
                  pl.BlockSpec((N_HEADS * NOPE, LANES), lambda b, pt: (0, 0)),
                  pl.BlockSpec(memory_space=pl.ANY), pl.BlockSpec(memory_space=pl.ANY)],
        out_specs=per_seq(nq, LANES),
        scratch_shapes=[
            pltpu.VMEM((N_HEADS * NOPE + nq, LANES), BF16),
            pltpu.VMEM((page, LANES), F32), pltpu.VMEM((page, ROPE), F32),
            pltpu.VMEM((2 * nc, g, page, LANES), F32),
            pltpu.VMEM((2 * nc, g, ROPE, page), F32),
            pltpu.VMEM((2, gmax * page, LANES), BF16),
            pltpu.VMEM((2, nq, gmax * page), F32),
            pltpu.VMEM((2, nq, LANES), F32),
            pltpu.VMEM((2, LANES, 2 * page), BF16),
            pltpu.SemaphoreType.DMA((2, 2 * nc)),
            pltpu.VMEM((nq, LANES), F32), pltpu.VMEM((nq, LANES), F32), pltpu.VMEM((nq, LANES), F32),
        ],
    )
    return pl.pallas_call(
        functools.partial(_sample_attn_kernel, layer=layer, g=g, ts=ts, page=page, n_pages=n_pages),
        out_shape=jax.ShapeDtypeStruct((db, nq, LANES), F32),
        grid_spec=grid_spec,
        compiler_params=pltpu.CompilerParams(
            dimension_semantics=("arbitrary",), vmem_limit_bytes=VMEM_LIMIT),
        name="sample_attn",
    )(page_table.reshape(-1), qabs, qr, latn, krn, wukt, cache_latent, cache_krope_t)


def _sample_out_kernel(lo_ref, wuv_ref, gpc_ref, zmg_ref, x_ref, mod_ref, wout_ref, y_ref, *, db, ts, d):
    gate = mod_ref[:, 2 * d:3 * d]
    for t in range(ts):
        ym = jnp.concatenate(
            [_dot(lo_ref[:, (h * ts + t) * LANES:(h * ts + t + 1) * LANES].astype(BF16),
                  wuv_ref[:, h * LANES:(h + 1) * LANES]) for h in range(N_HEADS)], axis=1)
        ymg = (ym * zmg_ref[t * db:(t + 1) * db, :].astype(F32)).astype(BF16)
        o = _dot(gpc_ref[t * db:(t + 1) * db, :], wout_ref[0:512, :]) + _dot(ymg, wout_ref[512:1024, :])
        y_ref[:, t * d:(t + 1) * d] = x_ref[:, t * d:(t + 1) * d] + gate * o


def _sample_out(lat_out2, wuv, gpc, zmg, xs2, mod_s, wout, ts):
    db = xs2.shape[0]
    d = xs2.shape[1] // ts
    return pl.pallas_call(
        functools.partial(_sample_out_kernel, db=db, ts=ts, d=d),
        out_shape=jax.ShapeDtypeStruct(xs2.shape, F32),
        compiler_params=pltpu.CompilerParams(vmem_limit_bytes=VMEM_LIMIT),
        name="sample_out",
    )(lat_out2, wuv, gpc, zmg, xs2, mod_s, wout)


def _rope_tables(pos):
    inv_freq = 1.0 / (ROPE_THETA ** (jnp.arange(0, ROPE, 2, dtype=F32) / ROPE))
    ang = pos.astype(F32)[:, None] * inv_freq[None, :]
    cos, sin = jnp.cos(ang), jnp.sin(ang)
    n = pos.shape[0]
    half = ROPE // 2
    z = lambda w: jnp.zeros((n, w), F32)
    cos_t = jnp.concatenate([jnp.ones((n, NOPE), F32), cos, cos, jnp.ones((n, LANES - NOPE - ROPE), F32)], axis=1)
    sin_a = jnp.concatenate([z(NOPE), -sin, z(LANES - NOPE - half)], axis=1)
    sin_b = jnp.concatenate([z(NOPE + half), sin, z(LANES - NOPE - ROPE)], axis=1)
    return jnp.stack([cos_t, sin_a, sin_b])


def _all_layer_weights(norm_g, w_in, pool_w, pool_scale, conv_w, q_norm_g, w_uq, qn_g, qr_g,
                       kv_norm_g, kr_g, w_uk, kn_g, w_uv, w_out):
    depth, d, _ = w_in.shape
    wt = jnp.swapaxes(w_in, 1, 2)
    zr = lambda n: jnp.zeros((depth, n, d), F32)
    wt = jnp.concatenate([wt[:, 0:1920], zr(NOPE), wt[:, 1920:1952], zr(LANES - NOPE - ROPE), wt[:, 1952:]], axis=1)
    win = jnp.swapaxes(wt, 1, 2).astype(BF16)
    pad_head = lambda w, per: jnp.concatenate(
        [jnp.concatenate([w[:, :, h * per:(h + 1) * per],
                          jnp.zeros((depth, w.shape[1], LANES - per), F32)], axis=2)
         for h in range(N_HEADS)], axis=2)
    seg = jnp.arange(LANES)
    nope_m = (seg < NOPE).astype(F32)
    rope_m = ((seg >= NOPE) & (seg < NOPE + ROPE)).astype(F32)
    s128k = nope_m[:, None] * nope_m[None, :] / NOPE
    s128q = s128k + rope_m[:, None] * rope_m[None, :] / ROPE
    per_layer = lambda a: jnp.broadcast_to(a, (depth,) + a.shape)
    zl = lambda w: jnp.zeros((depth, w), F32)
    gw = pool_w.shape[2]
    poolw = jnp.zeros((depth, 256, 256), F32)
    for gi in range(pool_w.shape[1]):
        poolw = poolw.at[:, gi * gw:(gi + 1) * gw, gi * gw:(gi + 1) * gw].set(pool_w[:, gi])
    wukt = jnp.swapaxes(w_uk, 1, 2)
    wukt_pad = jnp.stack([jnp.concatenate([wukt[:, h * NOPE:(h + 1) * NOPE],
                                           jnp.zeros((depth, LANES - NOPE, w_uk.shape[1]), F32)], axis=1)
                          for h in range(N_HEADS)], axis=1)
    kngpad = jnp.concatenate([kn_g, zl(LANES - NOPE)], axis=1)[:, None, :]
    return {
        "norm_g": norm_g[:, None, :],
        "win": win,
        "poolw": poolw.astype(BF16),
        "pscale": pool_scale[:, None, :],
        "convw": conv_w,
        "qng": q_norm_g[:, None, :],
        "wuq": pad_head(w_uq, NOPE + ROPE).astype(BF16),
        "s128q": per_layer(s128q.astype(BF16)),
        "s128k": per_layer(s128k.astype(BF16)),
        "gq": (jnp.concatenate([qn_g, qr_g, zl(LANES - NOPE - ROPE)], axis=1) * SM_SCALE)[:, None, :],
        "kvg": kv_norm_g[:, None, :],
        "krg": jnp.concatenate([zl(NOPE), kr_g, zl(LANES - NOPE - ROPE)], axis=1)[:, None, :],
        "wuk": pad_head(w_uk, NOPE).astype(BF16),
        "gk": kngpad,
        "kngpad": kngpad,
        "wuv": w_uv.astype(BF16),
        "wukt": wukt.astype(BF16),
        "wukt_pad": wukt_pad.astype(BF16),
        "wout": w_out.astype(BF16),
    }


def kernel(x_prompt, x_sample, cache_latent, cache_krope, state_pool, state_conv, page_table,
           c_prompt, c_sample, norm_g, w_ada, b_ada, w_in, pool_w, pool_scale, conv_w,
           q_norm_g, w_uq, qn_g, qr_g, kv_norm_g, kr_g, w_uk, kn_g, w_uv, w_out):
    depth = w_in.shape[0]
    bp, seq, d = x_prompt.shape
    db, ts, _ = x_sample.shape
    page = cache_latent.shape[2]
    past_len = page_table.shape[1] * page
    ts_p = 512 if seq % 512 == 0 else seq
    tq = 512 if seq % 512 == 0 else seq
    g_pages = 64
    cache_krope_t = jnp.swapaxes(cache_krope, 2, 3)

    mod_prompt, mod_sample = _modulation(c_prompt, c_sample, w_ada, b_ada)
    tab_p = _rope_tables(jnp.arange(seq, dtype=jnp.int32))
    tab_s = jnp.repeat(_rope_tables(past_len + jnp.arange(ts, dtype=jnp.int32)), db, axis=1)
    stacked = _all_layer_weights(norm_g, w_in, pool_w, pool_scale, conv_w, q_norm_g, w_uq, qn_g, qr_g,
                                 kv_norm_g, kr_g, w_uk, kn_g, w_uv, w_out)
    lws = [{name: arr[l] for name, arr in stacked.items()} for l in range(depth)]

    xp = x_prompt
    lat_p, kr_p, pool_p, conv_p = [], [], [], []
    for l in range(depth):
        mod_p = mod_prompt[l].reshape(bp, 3, d)
        gpc, zmg, qp, kp, v, lat, kr, pst, cst = _prompt_in(xp, mod_p, lws[l], tab_p, ts_p)
        xp = _prompt_out(qp, kp, v, gpc, zmg, xp, mod_p, lws[l]["wout"], tq)
        lat_p.append(lat)
        kr_p.append(kr)
        pool_p.append(pst)
        conv_p.append(cst)

    xs2 = x_sample.reshape(db, ts * d)
    lat_s, kr_s, pool_s, conv_s = [], [], [], []
    for l in range(depth):
        mod_s = mod_sample[l]
        pst = jnp.swapaxes(state_pool[l], 0, 1)
        cst = state_conv[l].reshape(db, -1)
        gpc, zmg, qabs, qr, latn, krn, psto, csto = _sample_in(xs2, mod_s, lws[l], tab_s, pst, cst, ts, past_len)
        nq = N_HEADS * ts
        lat_out = _sample_attn(page_table, qabs.reshape(db, nq, LANES), qr.reshape(db, nq, LANES),
                               latn.reshape(db, ts, LANES), krn.reshape(db, ts, LANES), lws[l]["wukt"],
                               cache_latent, cache_krope_t, l, g_pages, ts)
        xs2 = _sample_out(lat_out.reshape(db, nq * LANES), lws[l]["wuv"], gpc, zmg, xs2, mod_s,
                          lws[l]["wout"], ts)
        lat_s.append(latn.reshape(db, ts, LANES))
        kr_s.append(krn.reshape(db, ts, LANES)[:, :, NOPE:NOPE + ROPE])
        pool_s.append(jnp.swapaxes(psto, 0, 1))
        conv_s.append(csto.reshape(state_conv.shape[1:]))

    return (xp, xs2.reshape(x_sample.shape),
            jnp.stack(lat_p), jnp.stack(kr_p), jnp.stack(pool_p), jnp.stack(conv_p),
            jnp.stack(lat_s), jnp.stack(kr_s), jnp.stack(pool_s), jnp.stack(conv_s))
```

```python
import functools
import math

import jax
import jax.numpy as jnp
from jax import lax
from jax.experimental import pallas as pl
from jax.experimental.pallas import tpu as pltpu

F32 = jnp.float32
BF16 = jnp.bfloat16

POOL_WINDOWS = (2, 4, 8, 16)
POOL_MAX = max(POOL_WINDOWS)
CONV_W = 3
N_HEADS = 4
NOPE = 64
ROPE = 32
ROPE_THETA = 10000.0
EPS = 1e-6
NEG = -1e30
SM_SCALE = 1.0 / math.sqrt(NOPE + ROPE)
LANES = 128
HALO = 16
VMEM_LIMIT = 56 * 1024 * 1024

C_POOL = 0
C_CONV = 512
C_MLA = 1536
C_ZM = 2048
D_IN_PAD = 2560


def _sigmoid(x):
    return 1.0 / (1.0 + jnp.exp(-x))


def _silu(x):
    return x * _sigmoid(x)


def _dot(a, b):
    return jnp.dot(a, b, preferred_element_type=F32)


def _dot_nt(a, b):
    return lax.dot_general(a, b, (((1,), (1,)), ((), ())), preferred_element_type=F32)


def _rms(x, n):
    return x * lax.rsqrt(jnp.sum(x * x, axis=-1, keepdims=True) * (1.0 / n) + EPS)


def _rope_block(x, cos_t, sin_a, sin_b):
    return x * cos_t + pltpu.roll(x, LANES - ROPE // 2, 1) * sin_a + pltpu.roll(x, ROPE // 2, 1) * sin_b


def _mod_kernel(cp_ref, cs_ref, w_ref, b_ref, op_ref, os_ref):
    w = w_ref[0].astype(BF16)
    op_ref[0] = _dot(_silu(cp_ref[...]).astype(BF16), w) + b_ref[0]
    os_ref[0] = _dot(_silu(cs_ref[...]).astype(BF16), w) + b_ref[0]


def _modulation(c_prompt, c_sample, w_ada, b_ada):
    depth, d, d3 = w_ada.shape
    bp, db = c_prompt.shape[0], c_sample.shape[0]
    nb = d3 // d
    return pl.pallas_call(
        _mod_kernel,
        out_shape=[jax.ShapeDtypeStruct((depth, bp, d3), F32), jax.ShapeDtypeStruct((depth, db, d3), F32)],
        grid=(depth, nb),
        in_specs=[
            pl.BlockSpec((bp, d), lambda l, j: (0, 0)),
            pl.BlockSpec((db, d), lambda l, j: (0, 0)),
            pl.BlockSpec((1, d, d), lambda l, j: (l, 0, j)),
            pl.BlockSpec((1, 1, d), lambda l, j: (l, 0, j)),
        ],
        out_specs=[pl.BlockSpec((1, bp, d), lambda l, j: (l, 0, j)),
                   pl.BlockSpec((1, db, d), lambda l, j: (l, 0, j))],
        compiler_params=pltpu.CompilerParams(vmem_limit_bytes=VMEM_LIMIT),
        name="adaln_mod",
    )(c_prompt, c_sample, w_ada, b_ada.reshape(depth, 1, d3))


def _mla_prep(pm, qng, wuq, s128q, gq, kvg, krg, cos_t, sin_a, sin_b):
    cq = pm[:, 0:256]
    ckv_raw = pm[:, 256:384]
    kr_raw = pm[:, 384:512]
    cqn = (_rms(cq, 256) * qng).astype(BF16)
    q = _dot(cqn, wuq)
    qh = []
    for h in range(N_HEADS):
        xh = q[:, h * LANES:(h + 1) * LANES]
        ms = _dot((xh * xh).astype(BF16), s128q)
        xh = xh * lax.rsqrt(ms + EPS) * gq
        qh.append(_rope_block(xh, cos_t, sin_a, sin_b))
    ckv = _rms(ckv_raw, 128) * kvg
    krn = _rms(kr_raw, ROPE) * krg
    kr = _rope_block(krn, cos_t, sin_a, sin_b)
    return qh, ckv, kr


def _prompt_in_kernel(x_ref, mod_ref, g_ref, win_ref, poolw_ref, pscale_ref, convw_ref,
                      qng_ref, wuq_ref, s128q_ref, gq_ref, kvg_ref, krg_ref,
                      wuk_ref, s128k_ref, gk_ref, wuv_ref, tab_ref,
                      gpc_ref, zmg_ref, qp_ref, kp_ref, v_ref, lat_ref, kr_ref, pst_ref, cst_ref,
                      uext, vext, *, ts, nt):
    t = pl.program_id(1)
    x = x_ref[0]
    ms = jnp.mean(x * x, axis=-1, keepdims=True)
    h = (x * lax.rsqrt(ms + EPS) * g_ref[...]) * (1.0 + mod_ref[0, 1:2, :]) + mod_ref[0, 0:1, :]
    hb = h.astype(BF16)

    @pl.when(t == 0)
    def _():
        uext[0:HALO, :] = jnp.zeros((HALO, 256), F32)
        vext[0:HALO, :] = jnp.zeros((HALO, 256), F32)

    pp = _dot(hb, win_ref[:, C_POOL:C_POOL + 512])
    u = pp[:, 0:256]
    zp = pp[:, 256:512]
    uext[HALO:HALO + ts, :] = u
    ext = uext[...]
    a1 = ext + pltpu.roll(ext, 1, 0)
    a2 = a1 + pltpu.roll(a1, 2, 0)
    a3 = a2[:, 128:256]
    a3 = a3 + pltpu.roll(a3, 4, 0)
    a4 = a3 + pltpu.roll(a3, 8, 0)
    pos1 = t * ts + lax.broadcasted_iota(jnp.int32, (ts, 1), 0) + 1
    inv = [1.0 / jnp.minimum(pos1, w).astype(F32) for w in POOL_WINDOWS]
    lane = lax.broadcasted_iota(jnp.int32, (1, LANES), 1)
    mean_l = jnp.where(lane < 64, a1[HALO:, 0:128] * inv[0], a2[HALO:, 0:128] * inv[1])
    mean_r = jnp.where(lane < 64, a3[HALO:] * inv[2], a4[HALO:] * inv[3])
    d = jnp.concatenate([mean_l, mean_r], axis=1) - u
    yp = _dot(d.astype(BF16), poolw_ref[...]) * pscale_ref[...]
    gp = yp * _silu(zp)
    uext[0:HALO, :] = u[ts - HALO:ts, :]

    pc = _dot(hb, win_ref[:, C_CONV:C_CONV + 1024])
    hc = pc[:, 0:256]
    bg = pc[:, 256:512]
    cg = pc[:, 512:768]
    zc = pc[:, 768:1024]
    v = cg * hc
    vext[HALO:HALO + ts, :] = v
    e = vext[...]
    y = convw_ref[0:1, :] * pltpu.roll(e, 2, 0) + convw_ref[1:2, :] * pltpu.roll(e, 1, 0) + convw_ref[2:3, :] * e
    gc = bg * y[HALO:] * _silu(zc)
    gpc_ref[0] = jnp.concatenate([gp, gc], axis=1).astype(BF16)
    vext[0:HALO, :] = v[ts - HALO:ts, :]

    zmg_ref[0] = _silu(_dot(hb, win_ref[:, C_ZM:C_ZM + 512])).astype(BF16)

    pm = _dot(hb, win_ref[:, C_MLA:C_MLA + 512])
    cos_t = tab_ref[0]
    sin_a = tab_ref[1]
    sin_b = tab_ref[2]
    qh, ckv, kr = _mla_prep(pm, qng_ref[...], wuq_ref[...], s128q_ref[...], gq_ref[...],
                            kvg_ref[...], krg_ref[...], cos_t, sin_a, sin_b)
    qp_ref[0] = jnp.concatenate(qh, axis=1).astype(BF16)
    lat_ref[0] = ckv
    kr_ref[0] = kr[:, NOPE:NOPE + ROPE]
    cb = ckv.astype(BF16)
    kc = _dot(cb, wuk_ref[...])
    kh = []
    for hh in range(N_HEADS):
        xh = kc[:, hh * LANES:(hh + 1) * LANES]
        msk = _dot((xh * xh).astype(BF16), s128k_ref[...])
        kh.append(xh * lax.rsqrt(msk + EPS) * gk_ref[...] + kr)
    kp_ref[0] = jnp.concatenate(kh, axis=1).astype(BF16)
    v_ref[0] = _dot(cb, wuv_ref[...]).astype(BF16)

    @pl.when(t == nt - 1)
    def _():
        pst_ref[0] = uext[pl.ds(HALO + ts - (POOL_MAX - 1), POOL_MAX - 1), :]
        cst_ref[0] = vext[pl.ds(HALO + ts - (CONV_W - 1), CONV_W - 1), :]


def _prompt_in(x, mod_p, lw, tab, ts):
    b, t, d = x.shape
    nt = t // ts
    const = lambda shape: pl.BlockSpec(shape, lambda i, j: (0,) * len(shape))
    tile = lambda w: pl.BlockSpec((1, ts, w), lambda i, j: (i, j, 0))
    outs = [
        jax.ShapeDtypeStruct((b, t, 512), BF16),
        jax.ShapeDtypeStruct((b, t, 512), BF16),
        jax.ShapeDtypeStruct((b, t, 512), BF16),
        jax.ShapeDtypeStruct((b, t, 512), BF16),
        jax.ShapeDtypeStruct((b, t, 512), BF16),
        jax.ShapeDtypeStruct((b, t, 128), F32),
        jax.ShapeDtypeStruct((b, t, ROPE), F32),
        jax.ShapeDtypeStruct((b, POOL_MAX - 1, 256), F32),
        jax.ShapeDtypeStruct((b, CONV_W - 1, 256), F32),
    ]
    return pl.pallas_call(
        functools.partial(_prompt_in_kernel, ts=ts, nt=nt),
        out_shape=outs,
        grid=(b, nt),
        in_specs=[
            tile(d),
            pl.BlockSpec((1, 3, d), lambda i, j: (i, 0, 0)),
            const((1, d)),
            const((d, D_IN_PAD)),
            const((256, 256)), const((1, 256)), const((CONV_W, 256)),
            const((1, 256)), const((256, 512)), const((128, 128)), const((1, 128)),
            const((1, 128)), const((1, 128)),
            const((128, 512)), const((128, 128)), const((1, 128)), const((128, 512)),
            pl.BlockSpec((3, ts, 128), lambda i, j: (0, j, 0)),
        ],
        out_specs=[
            tile(512), tile(512), tile(512), tile(512), tile(512), tile(128), tile(ROPE),
            pl.BlockSpec((1, POOL_MAX - 1, 256), lambda i, j: (i, 0, 0)),
            pl.BlockSpec((1, CONV_W - 1, 256), lambda i, j: (i, 0, 0)),
        ],
        scratch_shapes=[pltpu.VMEM((HALO + ts, 256), F32), pltpu.VMEM((HALO + ts, 256), F32)],
        compiler_params=pltpu.CompilerParams(
            dimension_semantics=("parallel", "arbitrary"), vmem_limit_bytes=VMEM_LIMIT),
        name="prompt_in",
    )(x, mod_p, lw["norm_g"], lw["win"], lw["poolw"], lw["pscale"], lw["convw"],
      lw["qng"], lw["wuq"], lw["s128q"], lw["gq"], lw["kvg"], lw["krg"],
      lw["wuk"], lw["s128k"], lw["gk"], lw["wuv"], tab)


def _prompt_out_kernel(q_ref, k_ref, v_ref, gpc_ref, zmg_ref, x_ref, gate_ref, wout_ref, y_ref,
                       s_scr, p_scr, ym_scr, *, tq, nt):
    i = pl.program_id(1)
    row = lax.broadcasted_iota(jnp.int32, (tq, LANES), 0)
    lane = lax.broadcasted_iota(jnp.int32, (tq, LANES), 1)
    n_diag = tq // LANES

    def attend(n_keys):
        n_pieces = n_keys // LANES
        for h in range(N_HEADS):
            par = h % 2
            sl = slice(h * LANES, (h + 1) * LANES)
            s_scr[par, :, 0:n_keys] = _dot_nt(q_ref[0, :, sl], k_ref[0, 0:n_keys, sl])

            def piece(t):
                x = s_scr[par, :, t * LANES:(t + 1) * LANES]
                td = t - (n_pieces - n_diag)
                if td >= 0:
                    x = jnp.where(td * LANES + lane <= row, x, NEG)
                return x

            mx = piece(0)
            for t in range(1, n_pieces):
                mx = jnp.maximum(mx, piece(t))
            m_rep = jnp.broadcast_to(jnp.max(mx, axis=-1, keepdims=True), (tq, LANES))
            psum = jnp.zeros((tq, LANES), F32)
            for t in range(n_pieces):
                p_t = jnp.exp(piece(t) - m_rep)
                psum = psum + p_t
                p_scr[par, :, t * LANES:(t + 1) * LANES] = p_t.astype(BF16)
            o = _dot(p_scr[par, :, 0:n_keys], v_ref[0, 0:n_keys, sl])
            ym_scr[:, sl] = o / jnp.sum(psum, axis=-1, keepdims=True)

    for v in range(nt):
        pl.when(i == v)(functools.partial(attend, (v + 1) * tq))

    ym = ym_scr[...] * zmg_ref[0].astype(F32)
    o = _dot(gpc_ref[0], wout_ref[0:512, :]) + _dot(ym.astype(BF16), wout_ref[512:1024, :])
    y_ref[0] = x_ref[0] + gate_ref[0, 2:3, :] * o


def _prompt_out(qp, kp, v, gpc, zmg, x, mod_p, wout, tq):
    b, t, d = x.shape
    tile = lambda w: pl.BlockSpec((1, tq, w), lambda i, j: (i, j, 0))
    full = lambda w: pl.BlockSpec((1, t, w), lambda i, j: (i, 0, 0))
    return pl.pallas_call(
        functools.partial(_prompt_out_kernel, tq=tq, nt=t // tq),
        out_shape=jax.ShapeDtypeStruct((b, t, d), F32),
        grid=(b, t // tq),
        in_specs=[
            tile(512), full(512), full(512), tile(512), tile(512), tile(d),
            pl.BlockSpec((1, 3, d), lambda i, j: (i, 0, 0)),
            pl.BlockSpec((d, d), lambda i, j: (0, 0)),
        ],
        out_specs=tile(d),
        scratch_shapes=[pltpu.VMEM((2, tq, t), F32), pltpu.VMEM((2, tq, t), BF16),
                        pltpu.VMEM((tq, N_HEADS * LANES), F32)],
        compiler_params=pltpu.CompilerParams(
            dimension_semantics=("parallel", "arbitrary"), vmem_limit_bytes=VMEM_LIMIT),
        name="prompt_out",
    )(qp, kp, v, gpc, zmg, x, mod_p, wout)


def _sample_in_kernel(x_ref, mod_ref, g_ref, win_ref, poolw_ref, pscale_ref, convw_ref,
                      qng_ref, wuq_ref, s128q_ref, gq_ref, kvg_ref, krg_ref, kng_ref, wukt_ref, tab_ref,
                      pst_ref, cst_ref,
                      gpc_ref, zmg_ref, qabs_ref, qr_ref, lat_ref, krp_ref, psto_ref, csto_ref,
                      *, db, ts, d, past_len):
    shift = mod_ref[:, 0:d]
    scale1 = 1.0 + mod_ref[:, d:2 * d]
    hs = []
    for t in range(ts):
        x = x_ref[:, t * d:(t + 1) * d]
        ms = jnp.mean(x * x, axis=-1, keepdims=True)
        hs.append(((x * lax.rsqrt(ms + EPS) * g_ref[...]) * scale1 + shift).astype(BF16))
    hb = jnp.concatenate(hs, axis=0)
    rows = lambda a, t: a[t * db:(t + 1) * db]

    pp = _dot(hb, win_ref[:, C_POOL:C_POOL + 512])
    zp = pp[:, 256:512]
    np_ = POOL_MAX - 1
    ext = [pst_ref[k] for k in range(np_)] + [rows(pp, t)[:, 0:256] for t in range(ts)]
    for k in range(np_):
        psto_ref[k] = ext[ts + k]

    def doubled(prev, step, lo):
        return {r: prev[r] + prev[r - step] for r in range(lo, np_ + ts) if r in prev and (r - step) in prev}

    a0 = dict(enumerate(ext))
    a1 = doubled(a0, 1, 1)
    a2 = doubled(a1, 2, 3)
    a3 = doubled(a2, 4, 7)
    a4 = doubled(a3, 8, 15)
    lane = lax.broadcasted_iota(jnp.int32, (1, 256), 1)
    ds_ = []
    for t in range(ts):
        r = np_ + t
        inv = [1.0 / float(min(past_len + t + 1, w)) for w in POOL_WINDOWS]
        mean = jnp.where(lane < 64, a1[r] * inv[0],
                         jnp.where(lane < 128, a2[r] * inv[1],
                                   jnp.where(lane < 192, a3[r] * inv[2], a4[r] * inv[3])))
        ds_.append(mean - ext[r])
    dcat = jnp.concatenate(ds_, axis=0)
    gp = _dot(dcat.astype(BF16), poolw_ref[...]) * pscale_ref[...] * _silu(zp)

    pc = _dot(hb, win_ref[:, C_CONV:C_CONV + 1024])
    bg = pc[:, 256:512]
    zc = pc[:, 768:1024]
    vfull = pc[:, 512:768] * pc[:, 0:256]
    nc = CONV_W - 1
    vx = [cst_ref[:, k * 256:(k + 1) * 256] for k in range(nc)] + [rows(vfull, t) for t in range(ts)]
    for k in range(nc):
        csto_ref[:, k * 256:(k + 1) * 256] = vx[ts + k]
    ys = []
    for t in range(ts):
        acc = convw_ref[0:1, :] * vx[t]
        for k in range(1, CONV_W):
            acc = acc + convw_ref[k:k + 1, :] * vx[t + k]
        ys.append(acc)
    gc = bg * jnp.concatenate(ys, axis=0) * _silu(zc)
    gpc_ref[...] = jnp.concatenate([gp, gc], axis=1).astype(BF16)

    zmg_ref[...] = _silu(_dot(hb, win_ref[:, C_ZM:C_ZM + 512])).astype(BF16)

    pm = _dot(hb, win_ref[:, C_MLA:C_MLA + 512])
    qh, ckv, kr = _mla_prep(pm, qng_ref[...], wuq_ref[...], s128q_ref[...], gq_ref[...],
                            kvg_ref[...], krg_ref[...], tab_ref[0], tab_ref[1], tab_ref[2])
    for t in range(ts):
        lat_ref[:, t * LANES:(t + 1) * LANES] = rows(ckv, t)
        krp_ref[:, t * LANES:(t + 1) * LANES] = rows(kr, t)
    for h in range(N_HEADS):
        qa = _dot((qh[h] * kng_ref[...]).astype(BF16), wukt_ref[h])
        for t in range(ts):
            c0 = (h * ts + t) * LANES
            qabs_ref[:, c0:c0 + LANES] = rows(qa, t).astype(BF16)
            qr_ref[:, c0:c0 + LANES] = rows(qh[h], t).astype(BF16)


def _sample_in(xs2, mod_s, lw, tab, pst, cst, ts, past_len):
    db = xs2.shape[0]
    d = xs2.shape[1] // ts
    r = ts * db
    outs = [
        jax.ShapeDtypeStruct((r, 512), BF16),
        jax.ShapeDtypeStruct((r, 512), BF16),
        jax.ShapeDtypeStruct((db, N_HEADS * ts * LANES), BF16),
        jax.ShapeDtypeStruct((db, N_HEADS * ts * LANES), BF16),
        jax.ShapeDtypeStruct((db, ts * LANES), F32),
        jax.ShapeDtypeStruct((db, ts * LANES), F32),
        jax.ShapeDtypeStruct(pst.shape, F32),
        jax.ShapeDtypeStruct(cst.shape, F32),
    ]
    return pl.pallas_call(
        functools.partial(_sample_in_kernel, db=db, ts=ts, d=d, past_len=past_len),
        out_shape=outs,
        compiler_params=pltpu.CompilerParams(vmem_limit_bytes=VMEM_LIMIT),
        name="sample_in",
    )(xs2, mod_s, lw["norm_g"], lw["win"], lw["poolw"], lw["pscale"], lw["convw"],
      lw["qng"], lw["wuq"], lw["s128q"], lw["gq"], lw["kvg"], lw["krg"], lw["kngpad"], lw["wukt_pad"], tab,
      pst, cst)


def _sample_attn_kernel(pt_ref, qabs_ref, qr_ref, latn_ref, krn_ref, wukt_ref, lat_hbm, krt_hbm, out_ref,
                        lhs, newlat, newkr, lat_buf, krt_buf, latb, s_scr, mx_scr, lat_t, sem, m_s, l_s, acc_s,
                        *, layer, g, ts, page, n_pages):
    b = pl.program_id(0)
    nb = pl.num_programs(0)
    nc = n_pages // g
    nq = N_HEADS * ts
    nk = N_HEADS * NOPE
    PM = lat_t.shape[2] // page
    half = (b % 2) * nc
    other = nc - half

    def page_copies(seq, chunk, slot, j):
        pg = pt_ref[seq * n_pages + chunk * g + j]
        return (pltpu.make_async_copy(lat_hbm.at[layer, pg], lat_buf.at[slot, j], sem.at[0, slot]),
                pltpu.make_async_copy(krt_hbm.at[layer, pg], krt_buf.at[slot, j], sem.at[1, slot]))

    def start_page(seq, chunk, slot, j):
        lat_cp, krt_cp = page_copies(seq, chunk, slot, j)
        lat_cp.start(priority=j % 2)
        krt_cp.start(priority=(j + 1) % 2)

    def wait_chunk(slot):
        pltpu.make_async_copy(lat_hbm.at[layer, pl.ds(0, g)], lat_buf.at[slot], sem.at[0, slot]).wait()
        pltpu.make_async_copy(krt_hbm.at[layer, pl.ds(0, g)], krt_buf.at[slot], sem.at[1, slot]).wait()

    def scores(lat_bf, s_rope, lat_t_bf=None):
        res = _dot_nt(lhs[...], lat_bf) if lat_t_bf is None else _dot(lhs[...], lat_t_bf)
        kt = res[0:nk]
        k2 = kt * kt
        rr = []
        for h in range(N_HEADS):
            ss = jnp.sum(k2[h * NOPE:(h + 1) * NOPE], axis=0, keepdims=True)
            rr.append(jnp.broadcast_to(lax.rsqrt(ss * (1.0 / NOPE) + EPS), (ts, lat_bf.shape[0])))
        return res[nk:nk + nq] * jnp.concatenate(rr, axis=0) + s_rope

    def update(n_tiles, tile, s_max, lat_bf_all):
        m_old = m_s[...]
        m_new = jnp.maximum(m_old, s_max)
        corr = jnp.exp(m_old - m_new)
        psum = jnp.zeros((nq, LANES), F32)
        ps = []
        for t in range(n_tiles):
            p_t = jnp.exp(tile(t) - m_new)
            psum = psum + p_t
            ps.append(p_t.astype(BF16))
        l_s[...] = l_s[...] * corr + psum
        p = ps[0] if n_tiles == 1 else jnp.concatenate(ps, axis=1)
        acc_s[...] = acc_s[...] * corr + _dot(p, lat_bf_all)
        m_s[...] = m_new

    @pl.when(b == 0)
    def _():
        for c in range(nc):
            for j in range(g):
                start_page(0, c, c, j)
        lhs[0:nk, :] = wukt_ref[...]
        newlat[...] = jnp.zeros((page, LANES), F32)
        newkr[...] = jnp.zeros((page, ROPE), F32)

    for c in range(nc):
        wait_chunk(half + c)

    qr_bf = qr_ref[0].astype(F32)[:, NOPE:NOPE + ROPE].astype(BF16)
    lhs[nk:nk + nq, :] = qabs_ref[0]
    newlat[0:ts, :] = latn_ref[0]
    newkr[0:ts, :] = krn_ref[0][:, NOPE:NOPE + ROPE]
    m_s[...] = jnp.full((nq, LANES), NEG, F32)
    l_s[...] = jnp.zeros((nq, LANES), F32)
    acc_s[...] = jnp.zeros((nq, LANES), F32)
    lat_bf = newlat[...].astype(BF16)
    s = scores(lat_bf, _dot_nt(qr_bf, newkr[...].astype(BF16)))
    key = lax.broadcasted_iota(jnp.int32, (nq, page), 1)
    qi = lax.broadcasted_iota(jnp.int32, (nq, page), 0) % ts
    s = jnp.where(key <= qi, s, NEG)
    update(1, lambda t: s, jnp.broadcast_to(jnp.max(s, axis=-1, keepdims=True), (nq, LANES)), lat_bf)

    groups = [(0, (nc - 1) * g), ((nc - 1) * g, n_pages)] if nc > 1 else [(0, n_pages)]

    def head(gi):
        par = gi % 2
        lo, hi = groups[gi]
        parts = []
        for jj, p0 in enumerate(range(lo, hi, PM)):
            c, j0 = p0 // g, p0 % g
            for pn in range(2 * p0, min(2 * p0 + 2 * PM, n_pages)):
                @pl.when(b + 1 < nb)
                def _(cn=pn // g, j=pn % g):
                    start_page(b + 1, cn, other + cn, j)
            lat_f = lat_buf[half + c, j0:j0 + PM].reshape(PM * page, LANES)
            lat_bf = lat_f.astype(BF16)
            latb[par, PM * jj * page:(PM * jj + PM) * page, :] = lat_bf
            kr2 = jnp.concatenate([krt_buf[half + c, j0 + j] for j in range(PM)], axis=1).astype(BF16)
            lat_t[jj % 2] = lat_bf.T
            parts.append(scores(lat_bf, _dot(qr_bf, kr2), lat_t[jj % 2]))
        s_scr[par, :, 0:(hi - lo) * page] = jnp.concatenate(parts, axis=1)
        mx = parts[0]
        for part in parts[1:]:
            mx = jnp.maximum(mx, part)
        mx_scr[par] = jnp.broadcast_to(jnp.max(mx, axis=-1, keepdims=True), (nq, LANES))

    def tail(gi):
        par = gi % 2
        n_keys = (groups[gi][1] - groups[gi][0]) * page
        update(n_keys // LANES, lambda t: s_scr[par, :, t * LANES:(t + 1) * LANES], mx_scr[par],
               latb[par, 0:n_keys, :])

    for gi in range(len(groups)):
        if gi > 0:
            tail(gi - 1)
        head(gi)
    tail(len(groups) - 1)
    out_ref[0] = acc_s[...] / jnp.sum(l_s[...], axis=-1, keepdims=True)


def _sample_attn(page_table, qabs, qr, latn, krn, wukt, cache_latent, cache_krope_t, layer, g, ts):
    db, n_pages = page_table.shape
    page = cache_latent.shape[2]
    nc = n_pages // g
    assert n_pages % g == 0
    gmax = max(n_pages - g, g)
    nq = N_HEADS * ts
    per_seq = lambda rows, w: pl.BlockSpec((1, rows, w), lambda b, pt: (b, 0, 0))
    grid_spec = pltpu.PrefetchScalarGridSpec(
        num_scalar_prefetch=1,
        grid=(db,),
        in_specs=[per_seq(nq, LANES), per_seq(nq, LANES), per_seq(ts, LANES), per_seq(ts, LANES),
                  pl.BlockSpec((N_HEADS * NOPE, LANES), lambda b, pt: (0, 0)),
                  pl.BlockSpec(memory_space=pl.ANY), pl.BlockSpec(memory_space=pl.ANY)],
        out_specs=per_seq(nq, LANES),
        scratch_shapes=[
            pltpu.VMEM((N_HEADS * NOPE + nq, LANES), BF16),
            pltpu.VMEM((page, LANES), F32), pltpu.VMEM((page, ROPE), F32),
            pltpu.VMEM((2 * nc, g, page, LANES), F32),
            pltpu.VMEM((2 * nc, g, ROPE, page), F32),
            pltpu.VMEM((2, gmax * page, LANES), BF16),
            pltpu.VMEM((2, nq, gmax * page), F32),
            pltpu.VMEM((2, nq, LANES), F32),
            pltpu.VMEM((2, LANES, 2 * page), BF16),
            pltpu.SemaphoreType.DMA((2, 2 * nc)),
            pltpu.VMEM((nq, LANES), F32), pltpu.VMEM((nq, LANES), F32), pltpu.VMEM((nq, LANES), F32),
        ],
    )
    return pl.pallas_call(
        functools.partial(_sample_attn_kernel, layer=layer, g=g, ts=ts, page=page, n_pages=n_pages),
        out_shape=jax.ShapeDtypeStruct((db, nq, LANES), F32),
        grid_spec=grid_spec,
        compiler_params=pltpu.CompilerParams(
            dimension_semantics=("arbitrary",), vmem_limit_bytes=VMEM_LIMIT),
        name="sample_attn",
    )(page_table.reshape(-1), qabs, qr, latn, krn, wukt, cache_latent, cache_krope_t)


def _sample_out_kernel(lo_ref, wuv_ref, gpc_ref, zmg_ref, x_ref, mod_ref, wout_ref, y_ref, *, db, ts, d):
    gate = mod_ref[:, 2 * d:3 * d]
    for t in range(ts):
        ym = jnp.concatenate(
            [_dot(lo_ref[:, (h * ts + t) * LANES:(h * ts + t + 1) * LANES].astype(BF16),
                  wuv_ref[:, h * LANES:(h + 1) * LANES]) for h in range(N_HEADS)], axis=1)
        ymg = (ym * zmg_ref[t * db:(t + 1) * db, :].astype(F32)).astype(BF16)
        o = _dot(gpc_ref[t * db:(t + 1) * db, :], wout_ref[0:512, :]) + _dot(ymg, wout_ref[512:1024, :])
        y_ref[:, t * d:(t + 1) * d] = x_ref[:, t * d:(t + 1) * d] + gate * o


def _sample_out(lat_out2, wuv, gpc, zmg, xs2, mod_s, wout, ts):
    db = xs2.shape[0]
    d = xs2.shape[1] // ts
    return pl.pallas_call(
        functools.partial(_sample_out_kernel, db=db, ts=ts, d=d),
        out_shape=jax.ShapeDtypeStruct(xs2.shape, F32),
        compiler_params=pltpu.CompilerParams(vmem_limit_bytes=VMEM_LIMIT),
        name="sample_out",
    )(lat_out2, wuv, gpc, zmg, xs2, mod_s, wout)


def _rope_tables(pos):
    inv_freq = 1.0 / (ROPE_THETA ** (jnp.arange(0, ROPE, 2, dtype=F32) / ROPE))
    ang = pos.astype(F32)[:, None] * inv_freq[None, :]
    cos, sin = jnp.cos(ang), jnp.sin(ang)
    n = pos.shape[0]
    half = ROPE // 2
    z = lambda w: jnp.zeros((n, w), F32)
    cos_t = jnp.concatenate([jnp.ones((n, NOPE), F32), cos, cos, jnp.ones((n, LANES - NOPE - ROPE), F32)], axis=1)
    sin_a = jnp.concatenate([z(NOPE), -sin, z(LANES - NOPE - half)], axis=1)
    sin_b = jnp.concatenate([z(NOPE + half), sin, z(LANES - NOPE - ROPE)], axis=1)
    return jnp.stack([cos_t, sin_a, sin_b])


def _all_layer_weights(norm_g, w_in, pool_w, pool_scale, conv_w, q_norm_g, w_uq, qn_g, qr_g,
                       kv_norm_g, kr_g, w_uk, kn_g, w_uv, w_out):
    depth, d, _ = w_in.shape
    wt = jnp.swapaxes(w_in, 1, 2)
    zr = lambda n: jnp.zeros((depth, n, d), F32)
    wt = jnp.concatenate([wt[:, 0:1920], zr(NOPE), wt[:, 1920:1952], zr(LANES - NOPE - ROPE), wt[:, 1952:]], axis=1)
    win = jnp.swapaxes(wt, 1, 2).astype(BF16)
    pad_head = lambda w, per: jnp.concatenate(
        [jnp.concatenate([w[:, :, h * per:(h + 1) * per],
                          jnp.zeros((depth, w.shape[1], LANES - per), F32)], axis=2)
         for h in range(N_HEADS)], axis=2)
    seg = jnp.arange(LANES)
    nope_m = (seg < NOPE).astype(F32)
    rope_m = ((seg >= NOPE) & (seg < NOPE + ROPE)).astype(F32)
    s128k = nope_m[:, None] * nope_m[None, :] / NOPE
    s128q = s128k + rope_m[:, None] * rope_m[None, :] / ROPE
    per_layer = lambda a: jnp.broadcast_to(a, (depth,) + a.shape)
    zl = lambda w: jnp.zeros((depth, w), F32)
    gw = pool_w.shape[2]
    poolw = jnp.zeros((depth, 256, 256), F32)
    for gi in range(pool_w.shape[1]):
        poolw = poolw.at[:, gi * gw:(gi + 1) * gw, gi * gw:(gi + 1) * gw].set(pool_w[:, gi])
    wukt = jnp.swapaxes(w_uk, 1, 2)
    wukt_pad = jnp.stack([jnp.concatenate([wukt[:, h * NOPE:(h + 1) * NOPE],
                                           jnp.zeros((depth, LANES - NOPE, w_uk.shape[1]), F32)], axis=1)
                          for h in range(N_HEADS)], axis=1)
    kngpad = jnp.concatenate([kn_g, zl(LANES - NOPE)], axis=1)[:, None, :]
    return {
        "norm_g": norm_g[:, None, :],
        "win": win,
        "poolw": poolw.astype(BF16),
        "pscale": pool_scale[:, None, :],
        "convw": conv_w,
        "qng": q_norm_g[:, None, :],
        "wuq": pad_head(w_uq, NOPE + ROPE).astype(BF16),
        "s128q": per_layer(s128q.astype(BF16)),
        "s128k": per_layer(s128k.astype(BF16)),
        "gq": (jnp.concatenate([qn_g, qr_g, zl(LANES - NOPE - ROPE)], axis=1) * SM_SCALE)[:, None, :],
        "kvg": kv_norm_g[:, None, :],
        "krg": jnp.concatenate([zl(NOPE), kr_g, zl(LANES - NOPE - ROPE)], axis=1)[:, None, :],
        "wuk": pad_head(w_uk, NOPE).astype(BF16),
        "gk": kngpad,
        "kngpad": kngpad,
        "wuv": w_uv.astype(BF16),
        "wukt": wukt.astype(BF16),
        "wukt_pad": wukt_pad.astype(BF16),
        "wout": w_out.astype(BF16),
    }


def kernel(x_prompt, x_sample, cache_latent, cache_krope, state_pool, state_conv, page_table,
           c_prompt, c_sample, norm_g, w_ada, b_ada, w_in, pool_w, pool_scale, conv_w,
           q_norm_g, w_uq, qn_g, qr_g, kv_norm_g, kr_g, w_uk, kn_g, w_uv, w_out):
    depth = w_in.shape[0]
    bp, seq, d = x_prompt.shape
    db, ts, _ = x_sample.shape
    page = cache_latent.shape[2]
    past_len = page_table.shape[1] * page
    ts_p = 512 if seq % 512 == 0 else seq
    tq = 512 if seq % 512 == 0 else seq
    g_pages = 64
    cache_krope_t = jnp.swapaxes(cache_krope, 2, 3)

    mod_prompt, mod_sample = _modulation(c_prompt, c_sample, w_ada, b_ada)
    tab_p = _rope_tables(jnp.arange(seq, dtype=jnp.int32))
    tab_s = jnp.repeat(_rope_tables(past_len + jnp.arange(ts, dtype=jnp.int32)), db, axis=1)
    stacked = _all_layer_weights(norm_g, w_in, pool_w, pool_scale, conv_w, q_norm_g, w_uq, qn_g, qr_g,
                                 kv_norm_g, kr_g, w_uk, kn_g, w_uv, w_out)
    lws = [{name: arr[l] for name, arr in stacked.items()} for l in range(depth)]

    xp = x_prompt
    lat_p, kr_p, pool_p, conv_p = [], [], [], []
    for l in range(depth):
        mod_p = mod_prompt[l].reshape(bp, 3, d)
        gpc, zmg, qp, kp, v, lat, kr, pst, cst = _prompt_in(xp, mod_p, lws[l], tab_p, ts_p)
        xp = _prompt_out(qp, kp, v, gpc, zmg, xp, mod_p, lws[l]["wout"], tq)
        lat_p.append(lat)
        kr_p.append(kr)
        pool_p.append(pst)
        conv_p.append(cst)

    xs2 = x_sample.reshape(db, ts * d)
    lat_s, kr_s, pool_s, conv_s = [], [], [], []
    for l in range(depth):
        mod_s = mod_sample[l]
        pst = jnp.swapaxes(state_pool[l], 0, 1)
        cst = state_conv[l].reshape(db, -1)
        gpc, zmg, qabs, qr, latn, krn, psto, csto = _sample_in(xs2, mod_s, lws[l], tab_s, pst, cst, ts, past_len)
        nq = N_HEADS * ts
        lat_out = _sample_attn(page_table, qabs.reshape(db, nq, LANES), qr.reshape(db, nq, LANES),
                               latn.reshape(db, ts, LANES), krn.reshape(db, ts, LANES), lws[l]["wukt"],
                               cache_latent, cache_krope_t, l, g_pages, ts)
        xs2 = _sample_out(lat_out.reshape(db, nq * LANES), lws[l]["wuv"], gpc, zmg, xs2, mod_s,
                          lws[l]["wout"], ts)
        lat_s.append(latn.reshape(db, ts, LANES))
        kr_s.append(krn.reshape(db, ts, LANES)[:, :, NOPE:NOPE + ROPE])
        pool_s.append(jnp.swapaxes(psto, 0, 1))
        conv_s.append(csto.reshape(state_conv.shape[1:]))

    return (xp, xs2.reshape(x_sample.shape),
            jnp.stack(lat_p), jnp.stack(kr_p), jnp.stack(pool_p), jnp.stack(conv_p),
            jnp.stack(lat_s), jnp.stack(kr_s), jnp.stack(pool_s), jnp.stack(conv_s))
```

```python
import functools
import math

import jax
import jax.numpy as jnp
from jax import lax
from jax.experimental import pallas as pl
from jax.experimental.pallas import tpu as pltpu

F32 = jnp.float32
BF16 = jnp.bfloat16

POOL_WINDOWS = (2, 4, 8, 16)
POOL_MAX = max(POOL_WINDOWS)
CONV_W = 3
N_HEADS = 4
NOPE = 64
ROPE = 32
ROPE_THETA = 10000.0
EPS = 1e-6
NEG = -1e30
SM_SCALE = 1.0 / math.sqrt(NOPE + ROPE)
LANES = 128
HALO = 16
VMEM_LIMIT = 56 * 1024 * 1024

C_POOL = 0
C_CONV = 512
C_MLA = 1536
C_ZM = 2048
D_IN_PAD = 2560


def _sigmoid(x):
    return 1.0 / (1.0 + jnp.exp(-x))


def _silu(x):
    return x * _sigmoid(x)


def _dot(a, b):
    return jnp.dot(a, b, preferred_element_type=F32)


def _dot_nt(a, b):
    return lax.dot_general(a, b, (((1,), (1,)), ((), ())), preferred_element_type=F32)


def _rms(x, n):
    return x * lax.rsqrt(jnp.sum(x * x, axis=-1, keepdims=True) * (1.0 / n) + EPS)


def _rope_block(x, cos_t, sin_a, sin_b):
    return x * cos_t + pltpu.roll(x, LANES - ROPE // 2, 1) * sin_a + pltpu.roll(x, ROPE // 2, 1) * sin_b


def _mod_kernel(cp_ref, cs_ref, w_ref, b_ref, op_ref, os_ref):
    w = w_ref[0].astype(BF16)
    op_ref[0] = _dot(_silu(cp_ref[...]).astype(BF16), w) + b_ref[0]
    os_ref[0] = _dot(_silu(cs_ref[...]).astype(BF16), w) + b_ref[0]


def _modulation(c_prompt, c_sample, w_ada, b_ada):
    depth, d, d3 = w_ada.shape
    bp, db = c_prompt.shape[0], c_sample.shape[0]
    nb = d3 // d
    return pl.pallas_call(
        _mod_kernel,
        out_shape=[jax.ShapeDtypeStruct((depth, bp, d3), F32), jax.ShapeDtypeStruct((depth, db, d3), F32)],
        grid=(depth, nb),
        in_specs=[
            pl.BlockSpec((bp, d), lambda l, j: (0, 0)),
            pl.BlockSpec((db, d), lambda l, j: (0, 0)),
            pl.BlockSpec((1, d, d), lambda l, j: (l, 0, j)),
            pl.BlockSpec((1, 1, d), lambda l, j: (l, 0, j)),
        ],
        out_specs=[pl.BlockSpec((1, bp, d), lambda l, j: (l, 0, j)),
                   pl.BlockSpec((1, db, d), lambda l, j: (l, 0, j))],
        compiler_params=pltpu.CompilerParams(vmem_limit_bytes=VMEM_LIMIT),
        name="adaln_mod",
    )(c_prompt, c_sample, w_ada, b_ada.reshape(depth, 1, d3))


def _pair_segment_means(x, seg2):
    return jnp.concatenate(
        [_dot(jnp.square(x[:, p * 2 * LANES:(p + 1) * 2 * LANES]).astype(BF16), seg2)
         for p in range(x.shape[1] // (2 * LANES))], axis=1)


def _mla_prep(pm, qng, wuq, s128q, gq, kvg, krg, cos_t, sin_a, sin_b):
    cq = pm[:, 0:256]
    ckv_raw = pm[:, 256:384]
    kr_raw = pm[:, 384:512]
    cqn = (_rms(cq, 256) * qng).astype(BF16)
    q = _dot(cqn, wuq)
    ms_all = _pair_segment_means(q, s128q)
    qh = []
    for h in range(N_HEADS):
        xh = q[:, h * LANES:(h + 1) * LANES]
        xh = xh * lax.rsqrt(ms_all[:, h * LANES:(h + 1) * LANES] + EPS) * gq
        qh.append(_rope_block(xh, cos_t, sin_a, sin_b))
    ckv = _rms(ckv_raw, 128) * kvg
    krn = _rms(kr_raw, ROPE) * krg
    kr = _rope_block(krn, cos_t, sin_a, sin_b)
    return qh, ckv, kr


def _prompt_in_kernel(x_ref, mod_ref, g_ref, win_ref, poolw_ref, pscale_ref, convw_ref,
                      qng_ref, wuq_ref, s128q_ref, gq_ref, kvg_ref, krg_ref,
                      wuk_ref, s128k_ref, gk_ref, wuv_ref, tab_ref,
                      gpc_ref, zmg_ref, qp_ref, kp_ref, v_ref, lat_ref, kr_ref, pst_ref, cst_ref,
                      uext, vext, *, ts, nt):
    t = pl.program_id(1)
    x = x_ref[0]
    ms = jnp.mean(x * x, axis=-1, keepdims=True)
    h = (x * lax.rsqrt(ms + EPS) * g_ref[...]) * (1.0 + mod_ref[0, 1:2, :]) + mod_ref[0, 0:1, :]
    hb = h.astype(BF16)

    @pl.when(t == 0)
    def _():
        uext[0:HALO, :] = jnp.zeros((HALO, 256), F32)
        vext[0:HALO, :] = jnp.zeros((HALO, 256), F32)

    pp = _dot(hb, win_ref[:, C_POOL:C_POOL + 512])
    u = pp[:, 0:256]
    zp = pp[:, 256:512]
    uext[HALO:HALO + ts, :] = u
    ext = uext[...]
    a1 = ext + pltpu.roll(ext, 1, 0)
    a2 = a1 + pltpu.roll(a1, 2, 0)
    a3 = a2[:, 128:256]
    a3 = a3 + pltpu.roll(a3, 4, 0)
    a4 = a3 + pltpu.roll(a3, 8, 0)
    pos1 = t * ts + lax.broadcasted_iota(jnp.int32, (ts, 1), 0) + 1
    inv = [1.0 / jnp.minimum(pos1, w).astype(F32) for w in POOL_WINDOWS]
    lane = lax.broadcasted_iota(jnp.int32, (1, LANES), 1)
    mean_l = jnp.where(lane < 64, a1[HALO:, 0:128] * inv[0], a2[HALO:, 0:128] * inv[1])
    mean_r = jnp.where(lane < 64, a3[HALO:] * inv[2], a4[HALO:] * inv[3])
    d = jnp.concatenate([mean_l, mean_r], axis=1) - u
    yp = _dot(d.astype(BF16), poolw_ref[...]) * pscale_ref[...]
    gp = yp * _silu(zp)
    uext[0:HALO, :] = u[ts - HALO:ts, :]

    pc = _dot(hb, win_ref[:, C_CONV:C_CONV + 1024])
    hc = pc[:, 0:256]
    bg = pc[:, 256:512]
    cg = pc[:, 512:768]
    zc = pc[:, 768:1024]
    v = cg * hc
    vext[HALO:HALO + ts, :] = v
    e = vext[...]
    y = convw_ref[0:1, :] * pltpu.roll(e, 2, 0) + convw_ref[1:2, :] * pltpu.roll(e, 1, 0) + convw_ref[2:3, :] * e
    gc = bg * y[HALO:] * _silu(zc)
    gpc_ref[0] = jnp.concatenate([gp, gc], axis=1).astype(BF16)
    vext[0:HALO, :] = v[ts - HALO:ts, :]

    zmg_ref[0] = _silu(_dot(hb, win_ref[:, C_ZM:C_ZM + 512])).astype(BF16)

    pm = _dot(hb, win_ref[:, C_MLA:C_MLA + 512])
    cos_t = tab_ref[0]
    sin_a = tab_ref[1]
    sin_b = tab_ref[2]
    qh, ckv, kr = _mla_prep(pm, qng_ref[...], wuq_ref[...], s128q_ref[...], gq_ref[...],
                            kvg_ref[...], krg_ref[...], cos_t, sin_a, sin_b)
    qp_ref[0] = jnp.concatenate(qh, axis=1).astype(BF16)
    lat_ref[0] = ckv
    kr_ref[0] = kr[:, NOPE:NOPE + ROPE]
    cb = ckv.astype(BF16)
    kc = _dot(cb, wuk_ref[...])
    msk_all = _pair_segment_means(kc, s128k_ref[...])
    kh = []
    for hh in range(N_HEADS):
        sl = slice(hh * LANES, (hh + 1) * LANES)
        kh.append(kc[:, sl] * lax.rsqrt(msk_all[:, sl] + EPS) * gk_ref[...] + kr)
    kp_ref[0] = jnp.concatenate(kh, axis=1).astype(BF16)
    v_ref[0] = _dot(cb, wuv_ref[...]).astype(BF16)

    @pl.when(t == nt - 1)
    def _():
        pst_ref[0] = uext[pl.ds(HALO + ts - (POOL_MAX - 1), POOL_MAX - 1), :]
        cst_ref[0] = vext[pl.ds(HALO + ts - (CONV_W - 1), CONV_W - 1), :]


def _prompt_in(x, mod_p, lw, tab, ts):
    b, t, d = x.shape
    nt = t // ts
    const = lambda shape: pl.BlockSpec(shape, lambda i, j: (0,) * len(shape))
    tile = lambda w: pl.BlockSpec((1, ts, w), lambda i, j: (i, j, 0))
    outs = [
        jax.ShapeDtypeStruct((b, t, 512), BF16),
        jax.ShapeDtypeStruct((b, t, 512), BF16),
        jax.ShapeDtypeStruct((b, t, 512), BF16),
        jax.ShapeDtypeStruct((b, t, 512), BF16),
        jax.ShapeDtypeStruct((b, t, 512), BF16),
        jax.ShapeDtypeStruct((b, t, 128), F32),
        jax.ShapeDtypeStruct((b, t, ROPE), F32),
        jax.ShapeDtypeStruct((b, POOL_MAX - 1, 256), F32),
        jax.ShapeDtypeStruct((b, CONV_W - 1, 256), F32),
    ]
    return pl.pallas_call(
        functools.partial(_prompt_in_kernel, ts=ts, nt=nt),
        out_shape=outs,
        grid=(b, nt),
        in_specs=[
            tile(d),
            pl.BlockSpec((1, 3, d), lambda i, j: (i, 0, 0)),
            const((1, d)),
            const((d, D_IN_PAD)),
            const((256, 256)), const((1, 256)), const((CONV_W, 256)),
            const((1, 256)), const((256, 512)), const((256, 256)), const((1, 128)),
            const((1, 128)), const((1, 128)),
            const((128, 512)), const((256, 256)), const((1, 128)), const((128, 512)),
            pl.BlockSpec((3, ts, 128), lambda i, j: (0, j, 0)),
        ],
        out_specs=[
            tile(512), tile(512), tile(512), tile(512), tile(512), tile(128), tile(ROPE),
            pl.BlockSpec((1, POOL_MAX - 1, 256), lambda i, j: (i, 0, 0)),
            pl.BlockSpec((1, CONV_W - 1, 256), lambda i, j: (i, 0, 0)),
        ],
        scratch_shapes=[pltpu.VMEM((HALO + ts, 256), F32), pltpu.VMEM((HALO + ts, 256), F32)],
        compiler_params=pltpu.CompilerParams(
            dimension_semantics=("parallel", "arbitrary"), vmem_limit_bytes=VMEM_LIMIT),
        name="prompt_in",
    )(x, mod_p, lw["norm_g"], lw["win"], lw["poolw"], lw["pscale"], lw["convw"],
      lw["qng"], lw["wuq"], lw["s128q"], lw["gq"], lw["kvg"], lw["krg"],
      lw["wuk"], lw["s128k"], lw["gk"], lw["wuv"], tab)


def _prompt_out_kernel(q_ref, k_ref, v_ref, gpc_ref, zmg_ref, x_ref, gate_ref, wout_ref, y_ref,
                       s_scr, p_scr, ym_scr, *, tq, nt):
    i = pl.program_id(1)
    row = lax.broadcasted_iota(jnp.int32, (tq, LANES), 0)
    lane = lax.broadcasted_iota(jnp.int32, (tq, LANES), 1)
    n_diag = tq // LANES

    def attend(n_keys):
        n_pieces = n_keys // LANES
        for h in range(N_HEADS):
            par = h % 2
            sl = slice(h * LANES, (h + 1) * LANES)
            s_scr[par, :, 0:n_keys] = _dot_nt(q_ref[0, :, sl], k_ref[0, 0:n_keys, sl])

            def piece(t):
                x = s_scr[par, :, t * LANES:(t + 1) * LANES]
                td = t - (n_pieces - n_diag)
                if td >= 0:
                    x = jnp.where(td * LANES + lane <= row, x, NEG)
                return x

            mx = piece(0)
            for t in range(1, n_pieces):
                mx = jnp.maximum(mx, piece(t))
            m_rep = jnp.broadcast_to(jnp.max(mx, axis=-1, keepdims=True), (tq, LANES))
            psum = jnp.zeros((tq, LANES), F32)
            for t in range(n_pieces):
                p_t = jnp.exp(piece(t) - m_rep)
                psum = psum + p_t
                p_scr[par, :, t * LANES:(t + 1) * LANES] = p_t.astype(BF16)
            o = _dot(p_scr[par, :, 0:n_keys], v_ref[0, 0:n_keys, sl])
            ym_scr[:, sl] = o / jnp.sum(psum, axis=-1, keepdims=True)

    for v in range(nt):
        pl.when(i == v)(functools.partial(attend, (v + 1) * tq))

    ym = ym_scr[...] * zmg_ref[0].astype(F32)
    o = _dot(gpc_ref[0], wout_ref[0:512, :]) + _dot(ym.astype(BF16), wout_ref[512:1024, :])
    y_ref[0] = x_ref[0] + gate_ref[0, 2:3, :] * o


def _prompt_out(qp, kp, v, gpc, zmg, x, mod_p, wout, tq):
    b, t, d = x.shape
    tile = lambda w: pl.BlockSpec((1, tq, w), lambda i, j: (i, j, 0))
    full = lambda w: pl.BlockSpec((1, t, w), lambda i, j: (i, 0, 0))
    return pl.pallas_call(
        functools.partial(_prompt_out_kernel, tq=tq, nt=t // tq),
        out_shape=jax.ShapeDtypeStruct((b, t, d), F32),
        grid=(b, t // tq),
        in_specs=[
            tile(512), full(512), full(512), tile(512), tile(512), tile(d),
            pl.BlockSpec((1, 3, d), lambda i, j: (i, 0, 0)),
            pl.BlockSpec((d, d), lambda i, j: (0, 0)),
        ],
        out_specs=tile(d),
        scratch_shapes=[pltpu.VMEM((2, tq, t), F32), pltpu.VMEM((2, tq, t), BF16),
                        pltpu.VMEM((tq, N_HEADS * LANES), F32)],
        compiler_params=pltpu.CompilerParams(
            dimension_semantics=("parallel", "arbitrary"), vmem_limit_bytes=VMEM_LIMIT),
        name="prompt_out",
    )(qp, kp, v, gpc, zmg, x, mod_p, wout)


def _sample_in_kernel(x_ref, mod_ref, g_ref, win_ref, poolw_ref, pscale_ref, convw_ref,
                      qng_ref, wuq_ref, s128q_ref, gq_ref, kvg_ref, krg_ref, kng_ref, wukt_ref, tab_ref,
                      pst_ref, cst_ref,
                      gpc_ref, zmg_ref, qabs_ref, qr_ref, lat_ref, krp_ref, psto_ref, csto_ref,
                      *, db, ts, d, past_len):
    shift = mod_ref[:, 0:d]
    scale1 = 1.0 + mod_ref[:, d:2 * d]
    hs = []
    for t in range(ts):
        x = x_ref[:, t * d:(t + 1) * d]
        ms = jnp.mean(x * x, axis=-1, keepdims=True)
        hs.append(((x * lax.rsqrt(ms + EPS) * g_ref[...]) * scale1 + shift).astype(BF16))
    hb = jnp.concatenate(hs, axis=0)
    rows = lambda a, t: a[t * db:(t + 1) * db]

    pp = _dot(hb, win_ref[:, C_POOL:C_POOL + 512])
    zp = pp[:, 256:512]
    np_ = POOL_MAX - 1
    ext = [pst_ref[k] for k in range(np_)] + [rows(pp, t)[:, 0:256] for t in range(ts)]
    for k in range(np_):
        psto_ref[k] = ext[ts + k]

    def doubled(prev, step, lo):
        return {r: prev[r] + prev[r - step] for r in range(lo, np_ + ts) if r in prev and (r - step) in prev}

    a0 = dict(enumerate(ext))
    a1 = doubled(a0, 1, 1)
    a2 = doubled(a1, 2, 3)
    a3 = doubled(a2, 4, 7)
    a4 = doubled(a3, 8, 15)
    lane = lax.broadcasted_iota(jnp.int32, (1, 256), 1)
    ds_ = []
    for t in range(ts):
        r = np_ + t
        inv = [1.0 / float(min(past_len + t + 1, w)) for w in POOL_WINDOWS]
        mean = jnp.where(lane < 64, a1[r] * inv[0],
                         jnp.where(lane < 128, a2[r] * inv[1],
                                   jnp.where(lane < 192, a3[r] * inv[2], a4[r] * inv[3])))
        ds_.append(mean - ext[r])
    dcat = jnp.concatenate(ds_, axis=0)
    gp = _dot(dcat.astype(BF16), poolw_ref[...]) * pscale_ref[...] * _silu(zp)

    pc = _dot(hb, win_ref[:, C_CONV:C_CONV + 1024])
    bg = pc[:, 256:512]
    zc = pc[:, 768:1024]
    vfull = pc[:, 512:768] * pc[:, 0:256]
    nc = CONV_W - 1
    vx = [cst_ref[:, k * 256:(k + 1) * 256] for k in range(nc)] + [rows(vfull, t) for t in range(ts)]
    for k in range(nc):
        csto_ref[:, k * 256:(k + 1) * 256] = vx[ts + k]
    ys = []
    for t in range(ts):
        acc = convw_ref[0:1, :] * vx[t]
        for k in range(1, CONV_W):
            acc = acc + convw_ref[k:k + 1, :] * vx[t + k]
        ys.append(acc)
    gc = bg * jnp.concatenate(ys, axis=0) * _silu(zc)
    gpc_ref[...] = jnp.concatenate([gp, gc], axis=1).astype(BF16)

    zmg_ref[...] = _silu(_dot(hb, win_ref[:, C_ZM:C_ZM + 512])).astype(BF16)

    pm = _dot(hb, win_ref[:, C_MLA:C_MLA + 512])
    qh, ckv, kr = _mla_prep(pm, qng_ref[...], wuq_ref[...], s128q_ref[...], gq_ref[...],
                            kvg_ref[...], krg_ref[...], tab_ref[0], tab_ref[1], tab_ref[2])
    for t in range(ts):
        lat_ref[:, t * LANES:(t + 1) * LANES] = rows(ckv, t)
        krp_ref[:, t * LANES:(t + 1) * LANES] = rows(kr, t)
    for h in range(N_HEADS):
        qa = _dot((qh[h] * kng_ref[...]).astype(BF16), wukt_ref[h])
        for t in range(ts):
            c0 = (h * ts + t) * LANES
            qabs_ref[:, c0:c0 + LANES] = rows(qa, t).astype(BF16)
            qr_ref[:, c0:c0 + LANES] = rows(qh[h], t).astype(BF16)


def _sample_in(xs2, mod_s, lw, tab, pst, cst, ts, past_len):
    db = xs2.shape[0]
    d = xs2.shape[1] // ts
    r = ts * db
    outs = [
        jax.ShapeDtypeStruct((r, 512), BF16),
        jax.ShapeDtypeStruct((r, 512), BF16),
        jax.ShapeDtypeStruct((db, N_HEADS * ts * LANES), BF16),
        jax.ShapeDtypeStruct((db, N_HEADS * ts * LANES), BF16),
        jax.ShapeDtypeStruct((db, ts * LANES), F32),
        jax.ShapeDtypeStruct((db, ts * LANES), F32),
        jax.ShapeDtypeStruct(pst.shape, F32),
        jax.ShapeDtypeStruct(cst.shape, F32),
    ]
    return pl.pallas_call(
        functools.partial(_sample_in_kernel, db=db, ts=ts, d=d, past_len=past_len),
        out_shape=outs,
        compiler_params=pltpu.CompilerParams(vmem_limit_bytes=VMEM_LIMIT),
        name="sample_in",
    )(xs2, mod_s, lw["norm_g"], lw["win"], lw["poolw"], lw["pscale"], lw["convw"],
      lw["qng"], lw["wuq"], lw["s128q"], lw["gq"], lw["kvg"], lw["krg"], lw["kngpad"], lw["wukt_pad"], tab,
      pst, cst)


def _sample_attn_kernel(pt_ref, qabs_ref, qr_ref, latn_ref, krn_ref, wukt_ref, lat_hbm, krt_hbm, out_ref,
                        lhs, newlat, newkr, lat_buf, krt_buf, latb, s_scr, mx_scr, lat_t, sem, m_s, l_s, acc_s,
                        *, layer, g, ts, page, n_pages):
    b = pl.program_id(0)
    nb = pl.num_programs(0)
    nc = n_pages // g
    nq = N_HEADS * ts
    nk = N_HEADS * NOPE
    PM = lat_t.shape[2] // page
    half = (b % 2) * nc
    other = nc - half

    def page_copies(seq, chunk, slot, j):
        pg = pt_ref[seq * n_pages + chunk * g + j]
        return (pltpu.make_async_copy(lat_hbm.at[layer, pg], lat_buf.at[slot, j], sem.at[0, slot]),
                pltpu.make_async_copy(krt_hbm.at[layer, pg], krt_buf.at[slot, j], sem.at[1, slot]))

    def wait_chunk(slot):
        pltpu.make_async_copy(lat_hbm.at[layer, pl.ds(0, g)], lat_buf.at[slot], sem.at[0, slot]).wait()
        pltpu.make_async_copy(krt_hbm.at[layer, pl.ds(0, g)], krt_buf.at[slot], sem.at[1, slot]).wait()

    def scores(lat_bf, s_rope, lat_t_bf=None):
        res = _dot_nt(lhs[...], lat_bf) if lat_t_bf is None else _dot(lhs[...], lat_t_bf)
        kt = res[0:nk]
        k2 = kt * kt
        rr = []
        for h in range(N_HEADS):
            ss = jnp.sum(k2[h * NOPE:(h + 1) * NOPE], axis=0, keepdims=True)
            rr.append(jnp.broadcast_to(lax.rsqrt(ss * (1.0 / NOPE) + EPS), (ts, lat_bf.shape[0])))
        return res[nk:nk + nq] * jnp.concatenate(rr, axis=0) + s_rope

    def update(n_tiles, tile, s_max, lat_bf_all):
        m_old = m_s[...]
        m_new = jnp.maximum(m_old, s_max)
        corr = jnp.exp(m_old - m_new)
        psum = jnp.zeros((nq, LANES), F32)
        ps = []
        for t in range(n_tiles):
            p_t = jnp.exp(tile(t) - m_new)
            psum = psum + p_t
            ps.append(p_t.astype(BF16))
        l_s[...] = l_s[...] * corr + psum
        p = ps[0] if n_tiles == 1 else jnp.concatenate(ps, axis=1)
        acc_s[...] = acc_s[...] * corr + _dot(p, lat_bf_all)
        m_s[...] = m_new

    @pl.when(b == 0)
    def _():
        for c in range(nc):
            for j in range(g):
                for cp in page_copies(0, c, c, j):
                    cp.start()
        lhs[0:nk, :] = wukt_ref[...]
        newlat[...] = jnp.zeros((page, LANES), F32)
        newkr[...] = jnp.zeros((page, ROPE), F32)

    for c in range(nc):
        wait_chunk(half + c)

    qr_bf = qr_ref[0].astype(F32)[:, NOPE:NOPE + ROPE].astype(BF16)
    lhs[nk:nk + nq, :] = qabs_ref[0]
    newlat[0:ts, :] = latn_ref[0]
    newkr[0:ts, :] = krn_ref[0][:, NOPE:NOPE + ROPE]
    m_s[...] = jnp.full((nq, LANES), NEG, F32)
    l_s[...] = jnp.zeros((nq, LANES), F32)
    acc_s[...] = jnp.zeros((nq, LANES), F32)
    lat_bf = newlat[...].astype(BF16)
    s = scores(lat_bf, _dot_nt(qr_bf, newkr[...].astype(BF16)))
    key = lax.broadcasted_iota(jnp.int32, (nq, page), 1)
    qi = lax.broadcasted_iota(jnp.int32, (nq, page), 0) % ts
    s = jnp.where(key <= qi, s, NEG)
    update(1, lambda t: s, jnp.broadcast_to(jnp.max(s, axis=-1, keepdims=True), (nq, LANES)), lat_bf)

    groups = [(0, (nc - 1) * g), ((nc - 1) * g, n_pages)] if nc > 1 else [(0, n_pages)]

    def head(gi):
        par = gi % 2
        lo, hi = groups[gi]
        parts = []
        for jj, p0 in enumerate(range(lo, hi, PM)):
            c, j0 = p0 // g, p0 % g
            for pn in range(2 * p0, min(2 * p0 + 2 * PM, n_pages)):
                @pl.when(b + 1 < nb)
                def _(cn=pn // g, j=pn % g):
                    for cp in page_copies(b + 1, cn, other + cn, j):
                        cp.start()
            lat_f = lat_buf[half + c, j0:j0 + PM].reshape(PM * page, LANES)
            lat_bf = lat_f.astype(BF16)
            latb[par, PM * jj * page:(PM * jj + PM) * page, :] = lat_bf
            kr2 = jnp.concatenate([krt_buf[half + c, j0 + j] for j in range(PM)], axis=1).astype(BF16)
            lat_t[jj % 2] = lat_bf.T
            parts.append(scores(lat_bf, _dot(qr_bf, kr2), lat_t[jj % 2]))
        s_scr[par, :, 0:(hi - lo) * page] = jnp.concatenate(parts, axis=1)
        mx = parts[0]
        for part in parts[1:]:
            mx = jnp.maximum(mx, part)
        mx_scr[par] = jnp.broadcast_to(jnp.max(mx, axis=-1, keepdims=True), (nq, LANES))

    def tail(gi):
        par = gi % 2
        n_keys = (groups[gi][1] - groups[gi][0]) * page
        update(n_keys // LANES, lambda t: s_scr[par, :, t * LANES:(t + 1) * LANES], mx_scr[par],
               latb[par, 0:n_keys, :])

    for gi in range(len(groups)):
        if gi > 0:
            tail(gi - 1)
        head(gi)
    tail(len(groups) - 1)
    out_ref[0] = acc_s[...] / jnp.sum(l_s[...], axis=-1, keepdims=True)


def _sample_attn(page_table, qabs, qr, latn, krn, wukt, cache_latent, cache_krope_t, layer, g, ts):
    db, n_pages = page_table.shape
    page = cache_latent.shape[2]
    nc = n_pages // g
    assert n_pages % g == 0
    gmax = max(n_pages - g, g)
    nq = N_HEADS * ts
    per_seq = lambda rows, w: pl.BlockSpec((1, rows, w), lambda b, pt: (b, 0, 0))
    grid_spec = pltpu.PrefetchScalarGridSpec(
        num_scalar_prefetch=1,
        grid=(db,),
        in_specs=[per_seq(nq, LANES), per_seq(nq, LANES), per_seq(ts, LANES), per_seq(ts, LANES),
                  pl.BlockSpec((N_HEADS * NOPE, LANES), lambda b, pt: (0, 0)),
                  pl.BlockSpec(memory_space=pl.ANY), pl.BlockSpec(memory_space=pl.ANY)],
        out_specs=per_seq(nq, LANES),
        scratch_shapes=[
            pltpu.VMEM((N_HEADS * NOPE + nq, LANES), BF16),
            pltpu.VMEM((page, LANES), F32), pltpu.VMEM((page, ROPE), F32),
            pltpu.VMEM((2 * nc, g, page, LANES), F32),
            pltpu.VMEM((2 * nc, g, ROPE, page), F32),
            pltpu.VMEM((2, gmax * page, LANES), BF16),
            pltpu.VMEM((2, nq, gmax * page), F32),
            pltpu.VMEM((2, nq, LANES), F32),
            pltpu.VMEM((2, LANES, 2 * page), BF16),
            pltpu.SemaphoreType.DMA((2, 2 * nc)),
            pltpu.VMEM((nq, LANES), F32), pltpu.VMEM((nq, LANES), F32), pltpu.VMEM((nq, LANES), F32),
        ],
    )
    return pl.pallas_call(
        functools.partial(_sample_attn_kernel, layer=layer, g=g, ts=ts, page=page, n_pages=n_pages),
        out_shape=jax.ShapeDtypeStruct((db, nq, LANES), F32),
        grid_spec=grid_spec,
        compiler_params=pltpu.CompilerParams(
            dimension_semantics=("arbitrary",), vmem_limit_bytes=VMEM_LIMIT),
        name="sample_attn",
    )(page_table.reshape(-1), qabs, qr, latn, krn, wukt, cache_latent, cache_krope_t)


def _sample_out_kernel(lo_ref, wuv_ref, gpc_ref, zmg_ref, x_ref, mod_ref, wout_ref, y_ref, *, db, ts, d):
    gate = mod_ref[:, 2 * d:3 * d]
    for t in range(ts):
        ym = jnp.concatenate(
            [_dot(lo_ref[:, (h * ts + t) * LANES:(h * ts + t + 1) * LANES].astype(BF16),
                  wuv_ref[:, h * LANES:(h + 1) * LANES]) for h in range(N_HEADS)], axis=1)
        ymg = (ym * zmg_ref[t * db:(t + 1) * db, :].astype(F32)).astype(BF16)
        o = _dot(gpc_ref[t * db:(t + 1) * db, :], wout_ref[0:512, :]) + _dot(ymg, wout_ref[512:1024, :])
        y_ref[:, t * d:(t + 1) * d] = x_ref[:, t * d:(t + 1) * d] + gate * o


def _sample_out(lat_out2, wuv, gpc, zmg, xs2, mod_s, wout, ts):
    db = xs2.shape[0]
    d = xs2.shape[1] // ts
    return pl.pallas_call(
        functools.partial(_sample_out_kernel, db=db, ts=ts, d=d),
        out_shape=jax.ShapeDtypeStruct(xs2.shape, F32),
        compiler_params=pltpu.CompilerParams(vmem_limit_bytes=VMEM_LIMIT),
        name="sample_out",
    )(lat_out2, wuv, gpc, zmg, xs2, mod_s, wout)


def _rope_tables(pos):
    inv_freq = 1.0 / (ROPE_THETA ** (jnp.arange(0, ROPE, 2, dtype=F32) / ROPE))
    ang = pos.astype(F32)[:, None] * inv_freq[None, :]
    cos, sin = jnp.cos(ang), jnp.sin(ang)
    n = pos.shape[0]
    half = ROPE // 2
    z = lambda w: jnp.zeros((n, w), F32)
    cos_t = jnp.concatenate([jnp.ones((n, NOPE), F32), cos, cos, jnp.ones((n, LANES - NOPE - ROPE), F32)], axis=1)
    sin_a = jnp.concatenate([z(NOPE), -sin, z(LANES - NOPE - half)], axis=1)
    sin_b = jnp.concatenate([z(NOPE + half), sin, z(LANES - NOPE - ROPE)], axis=1)
    return jnp.stack([cos_t, sin_a, sin_b])


def _all_layer_weights(norm_g, w_in, pool_w, pool_scale, conv_w, q_norm_g, w_uq, qn_g, qr_g,
                       kv_norm_g, kr_g, w_uk, kn_g, w_uv, w_out):
    depth, d, _ = w_in.shape
    wt = jnp.swapaxes(w_in, 1, 2)
    zr = lambda n: jnp.zeros((depth, n, d), F32)
    wt = jnp.concatenate([wt[:, 0:1920], zr(NOPE), wt[:, 1920:1952], zr(LANES - NOPE - ROPE), wt[:, 1952:]], axis=1)
    win = jnp.swapaxes(wt, 1, 2).astype(BF16)
    pad_head = lambda w, per: jnp.concatenate(
        [jnp.concatenate([w[:, :, h * per:(h + 1) * per],
                          jnp.zeros((depth, w.shape[1], LANES - per), F32)], axis=2)
         for h in range(N_HEADS)], axis=2)
    seg = jnp.arange(LANES)
    nope_m = (seg < NOPE).astype(F32)
    rope_m = ((seg >= NOPE) & (seg < NOPE + ROPE)).astype(F32)
    s128k = nope_m[:, None] * nope_m[None, :] / NOPE
    s128q = s128k + rope_m[:, None] * rope_m[None, :] / ROPE
    per_layer = lambda a: jnp.broadcast_to(a, (depth,) + a.shape)
    zl = lambda w: jnp.zeros((depth, w), F32)
    gw = pool_w.shape[2]
    poolw = jnp.zeros((depth, 256, 256), F32)
    for gi in range(pool_w.shape[1]):
        poolw = poolw.at[:, gi * gw:(gi + 1) * gw, gi * gw:(gi + 1) * gw].set(pool_w[:, gi])
    wukt = jnp.swapaxes(w_uk, 1, 2)
    wukt_pad = jnp.stack([jnp.concatenate([wukt[:, h * NOPE:(h + 1) * NOPE],
                                           jnp.zeros((depth, LANES - NOPE, w_uk.shape[1]), F32)], axis=1)
                          for h in range(N_HEADS)], axis=1)
    kngpad = jnp.concatenate([kn_g, zl(LANES - NOPE)], axis=1)[:, None, :]
    return {
        "norm_g": norm_g[:, None, :],
        "win": win,
        "poolw": poolw.astype(BF16),
        "pscale": pool_scale[:, None, :],
        "convw": conv_w,
        "qng": q_norm_g[:, None, :],
        "wuq": pad_head(w_uq, NOPE + ROPE).astype(BF16),
        "s128q": per_layer(jnp.kron(jnp.eye(2, dtype=F32), s128q).astype(BF16)),
        "s128k": per_layer(jnp.kron(jnp.eye(2, dtype=F32), s128k).astype(BF16)),
        "gq": (jnp.concatenate([qn_g, qr_g, zl(LANES - NOPE - ROPE)], axis=1) * SM_SCALE)[:, None, :],
        "kvg": kv_norm_g[:, None, :],
        "krg": jnp.concatenate([zl(NOPE), kr_g, zl(LANES - NOPE - ROPE)], axis=1)[:, None, :],
        "wuk": pad_head(w_uk, NOPE).astype(BF16),
        "gk": kngpad,
        "kngpad": kngpad,
        "wuv": w_uv.astype(BF16),
        "wukt": wukt.astype(BF16),
        "wukt_pad": wukt_pad.astype(BF16),
        "wout": w_out.astype(BF16),
    }


def kernel(x_prompt, x_sample, cache_latent, cache_krope, state_pool, state_conv, page_table,
           c_prompt, c_sample, norm_g, w_ada, b_ada, w_in, pool_w, pool_scale, conv_w,
           q_norm_g, w_uq, qn_g, qr_g, kv_norm_g, kr_g, w_uk, kn_g, w_uv, w_out):
    depth = w_in.shape[0]
    bp, seq, d = x_prompt.shape
    db, ts, _ = x_sample.shape
    page = cache_latent.shape[2]
    past_len = page_table.shape[1] * page
    ts_p = 512 if seq % 512 == 0 else seq
    tq = 512 if seq % 512 == 0 else seq
    g_pages = 64
    cache_krope_t = jnp.swapaxes(cache_krope, 2, 3)

    mod_prompt, mod_sample = _modulation(c_prompt, c_sample, w_ada, b_ada)
    tab_p = _rope_tables(jnp.arange(seq, dtype=jnp.int32))
    tab_s = jnp.repeat(_rope_tables(past_len + jnp.arange(ts, dtype=jnp.int32)), db, axis=1)
    stacked = _all_layer_weights(norm_g, w_in, pool_w, pool_scale, conv_w, q_norm_g, w_uq, qn_g, qr_g,
                                 kv_norm_g, kr_g, w_uk, kn_g, w_uv, w_out)
    lws = [{name: arr[l] for name, arr in stacked.items()} for l in range(depth)]

    xp = x_prompt
    lat_p, kr_p, pool_p, conv_p = [], [], [], []
    for l in range(depth):
        mod_p = mod_prompt[l].reshape(bp, 3, d)
        gpc, zmg, qp, kp, v, lat, kr, pst, cst = _prompt_in(xp, mod_p, lws[l], tab_p, ts_p)
        xp = _prompt_out(qp, kp, v, gpc, zmg, xp, mod_p, lws[l]["wout"], tq)
        lat_p.append(lat)
        kr_p.append(kr)
        pool_p.append(pst)
        conv_p.append(cst)

    xs2 = x_sample.reshape(db, ts * d)
    lat_s, kr_s, pool_s, conv_s = [], [], [], []
    for l in range(depth):
        mod_s = mod_sample[l]
        pst = jnp.swapaxes(state_pool[l], 0, 1)
        cst = state_conv[l].reshape(db, -1)
        gpc, zmg, qabs, qr, latn, krn, psto, csto = _sample_in(xs2, mod_s, lws[l], tab_s, pst, cst, ts, past_len)
        nq = N_HEADS * ts
        lat_out = _sample_attn(page_table, qabs.reshape(db, nq, LANES), qr.reshape(db, nq, LANES),
                               latn.reshape(db, ts, LANES), krn.reshape(db, ts, LANES), lws[l]["wukt"],
                               cache_latent, cache_krope_t, l, g_pages, ts)
        xs2 = _sample_out(lat_out.reshape(db, nq * LANES), lws[l]["wuv"], gpc, zmg, xs2, mod_s,
                          lws[l]["wout"], ts)
        lat_s.append(latn.reshape(db, ts, LANES))
        kr_s.append(krn.reshape(db, ts, LANES)[:, :, NOPE:NOPE + ROPE])
        pool_s.append(jnp.swapaxes(psto, 0, 1))
        conv_s.append(csto.reshape(state_conv.shape[1:]))

    return (xp, xs2.reshape(x_sample.shape),
            jnp.stack(lat_p), jnp.stack(kr_p), jnp.stack(pool_p), jnp.stack(conv_p),
            jnp.stack(lat_s), jnp.stack(kr_s), jnp.stack(pool_s), jnp.stack(conv_s))
```
